```python
import math
import jax, jax.numpy as jnp
from jax import lax
import numpy as np

D_MODEL = 1024
BATCH = 4
SEQ = 8192
DEPTH = 1
DEC_BATCH = 8
DEC_SEQ = 32
PAST_LEN = 1024

CHUNK = 64
QBLOCK = 128
N_HEADS = 8
HEAD_DIM = 64
V_DIM = 2 * HEAD_DIM
D_ATTN = N_HEADS * V_DIM
D_QK = N_HEADS * 2 * HEAD_DIM
D_CONV = 1024
CONV_W = 3
D_IN_PROJ = 2 * D_QK + D_ATTN + 3 * D_CONV + 2 * D_MODEL
N_EXPERTS = 32
TOP_K = 4
D_FF = D_MODEL
SWIGLU_LIMIT = 7.0
SWIGLU_ALPHA = 1.702
EPS = 1e-6
NEG = -1e30

kernel_name = "hybrid_diffattn_shortconv_moe_stream_step"


def _rmsnorm(x, g):
    x32 = x.astype(jnp.float32)
    y = x32 * lax.rsqrt(jnp.mean(x32 * x32, axis=-1, keepdims=True) + EPS)
    return y.astype(x.dtype) * g.astype(x.dtype)


def _lambda_init(layer):
    return 0.8 - 0.6 * math.exp(-0.3 * layer)


def _split_proj(z):
    offs = [D_QK, 2 * D_QK, 2 * D_QK + D_ATTN,
            2 * D_QK + D_ATTN + D_CONV, 2 * D_QK + D_ATTN + 2 * D_CONV,
            2 * D_QK + D_ATTN + 3 * D_CONV, 2 * D_QK + D_ATTN + 3 * D_CONV + D_MODEL]
    q, k, v, cx, cb, cc, ga, gc = jnp.split(z, offs, axis=-1)
    lead = z.shape[:-1]
    q = q.reshape(*lead, N_HEADS, 2, HEAD_DIM)
    k = k.reshape(*lead, N_HEADS, 2, HEAD_DIM)
    v = v.reshape(*lead, N_HEADS, V_DIM)
    return q, k, v, cx, cb, cc, ga, gc


def _diff_attn(q, k, v, mask, lam):
    s = jnp.einsum('bqhmd,bkhmd->bhmqk', q.astype(jnp.float32), k.astype(jnp.float32)) * (HEAD_DIM ** -0.5)
    if mask is not None:
        s = jnp.where(mask, s, NEG)
    p = jax.nn.softmax(s, axis=-1)
    a = p[:, :, 0] - lam * p[:, :, 1]
    return jnp.einsum('bhqk,bkhe->bqhe', a, v.astype(jnp.float32))


def _mixer(xn, conv_prev, past_k, past_v, layer, w_in, lambda_q, lambda_k, g_subln,
           conv_w, w_attn_proj, w_conv_proj, w_out):
    bsz, t = xn.shape[0], xn.shape[1]
    z = xn @ w_in[layer]
    q, k, v, cx, cb, cc, ga, gc = _split_proj(z)
    lq = lambda_q[layer].astype(jnp.float32)
    lk = lambda_k[layer].astype(jnp.float32)
    lam_init = _lambda_init(layer)
    lam = jnp.exp(jnp.sum(lq[0] * lk[0])) - jnp.exp(jnp.sum(lq[1] * lk[1])) + lam_init

    if past_k is None:
        nb = t // QBLOCK
        qb = jnp.moveaxis(q.reshape(bsz, nb, QBLOCK, N_HEADS, 2, HEAD_DIM), 1, 0)
        k_chunk = jnp.arange(t) // CHUNK

        def blk(args):
            qi, bi = args
            q_chunk = (bi * QBLOCK + jnp.arange(QBLOCK)) // CHUNK
            mask = k_chunk[None, :] <= q_chunk[:, None]
            return _diff_attn(qi, k, v, mask, lam)

        o = lax.map(blk, (qb, jnp.arange(nb)))
        o = jnp.moveaxis(o, 0, 1).reshape(bsz, t, N_HEADS, V_DIM)
    else:
        pk = past_k.reshape(bsz, past_k.shape[1], N_HEADS, 2, HEAD_DIM).astype(k.dtype)
        k_all = jnp.concatenate([pk, k], axis=1)
        v_all = jnp.concatenate([past_v.astype(v.dtype), v], axis=1)
        o = _diff_attn(q, k_all, v_all, None, lam)

    o = _rmsnorm(o, g_subln[layer]) * (1.0 - lam_init)
    a_out = o.reshape(bsz, t, D_ATTN).astype(xn.dtype) @ w_attn_proj[layer]

    u = cc * cx
    padded = jnp.concatenate([conv_prev.astype(u.dtype), u], axis=1)
    cw = conv_w[layer]
    conv = padded[:, 0:t] * cw[0]
    for j in range(1, CONV_W):
        conv = conv + padded[:, j:j + t] * cw[j]
    c_out = (cb * conv) @ w_conv_proj[layer]
    new_conv = padded[:, -(CONV_W - 1):]

    merged = jax.nn.sigmoid(ga) * a_out + jax.nn.sigmoid(gc) * c_out
    mix = merged @ w_out[layer]
    k_rows = k.reshape(bsz, t, N_HEADS, 2 * HEAD_DIM)
    return mix, k_rows, v, new_conv


def _moe(xn, layer, w_router, b_router, w_gu, b_gu, w_down, b_down):
    bsz, t, d = xn.shape
    xf = xn.reshape(bsz * t, d)
    logits = (xf @ w_router[layer]).astype(jnp.float32) + b_router[layer].astype(jnp.float32)
    top_v, top_i = lax.top_k(logits, TOP_K)
    top_w = jax.nn.softmax(top_v, axis=-1)
    combine = jnp.sum(jax.nn.one_hot(top_i, N_EXPERTS, dtype=jnp.float32) * top_w[..., None], axis=-2)

    def expert(acc, ew):
        wgu, bgu, wd, bd, c = ew
        hgu = xf @ wgu + bgu
        gate = jnp.minimum(hgu[:, 0::2], SWIGLU_LIMIT)
        up = jnp.clip(hgu[:, 1::2], -SWIGLU_LIMIT, SWIGLU_LIMIT)
        glu = gate * jax.nn.sigmoid(SWIGLU_ALPHA * gate)
        out = (glu * (up + 1.0)) @ wd + bd
        return acc + c[:, None] * out.astype(jnp.float32), None

    acc0 = jnp.zeros((bsz * t, d), jnp.float32)
    acc, _ = lax.scan(expert, acc0, (w_gu[layer], b_gu[layer], w_down[layer], b_down[layer], combine.T))
    return acc.astype(xn.dtype).reshape(bsz, t, d)


def setup_inputs(seed: int = 0) -> dict:
    key = jax.random.key(seed)
    ks = jax.random.split(key, 24)

    def nrm(k, shape, scale):
        return jax.random.normal(k, shape, jnp.float32) * scale

    return {
        "x_prompt": nrm(ks[0], (BATCH, SEQ, D_MODEL), 1.0),
        "x_sample": nrm(ks[1], (DEC_BATCH, DEC_SEQ, D_MODEL), 1.0),
        "cache_k": nrm(ks[2], (DEPTH, DEC_BATCH, PAST_LEN, N_HEADS, 2 * HEAD_DIM), 1.0),
        "cache_v": nrm(ks[3], (DEPTH, DEC_BATCH, PAST_LEN, N_HEADS, V_DIM), 1.0),
        "state_conv": nrm(ks[4], (DEPTH, DEC_BATCH, CONV_W - 1, D_CONV), 1.0),
        "g_attn_norm": 1.0 + nrm(ks[5], (DEPTH, D_MODEL), 0.02),
        "w_in": nrm(ks[6], (DEPTH, D_MODEL, D_IN_PROJ), D_MODEL ** -0.5),
        "lambda_q": nrm(ks[7], (DEPTH, 2, HEAD_DIM), 0.1),
        "lambda_k": nrm(ks[8], (DEPTH, 2, HEAD_DIM), 0.1),
        "g_subln": 1.0 + nrm(ks[9], (DEPTH, V_DIM), 0.02),
        "conv_w": nrm(ks[10], (DEPTH, CONV_W, D_CONV), CONV_W ** -0.5),
        "w_attn_proj": nrm(ks[11], (DEPTH, D_ATTN, D_MODEL), D_ATTN ** -0.5),
        "w_conv_proj": nrm(ks[12], (DEPTH, D_CONV, D_MODEL), D_CONV ** -0.5),
        "w_out": nrm(ks[13], (DEPTH, D_MODEL, D_MODEL), D_MODEL ** -0.5),
        "g_ffn_norm": 1.0 + nrm(ks[14], (DEPTH, D_MODEL), 0.02),
        "w_router": nrm(ks[15], (DEPTH, D_MODEL, N_EXPERTS), D_MODEL ** -0.5),
        "b_router": nrm(ks[16], (DEPTH, N_EXPERTS), 0.01),
        "w_gu": nrm(ks[17], (DEPTH, N_EXPERTS, D_MODEL, 2 * D_FF), D_MODEL ** -0.5),
        "b_gu": nrm(ks[18], (DEPTH, N_EXPERTS, 2 * D_FF), 0.01),
        "w_down": nrm(ks[19], (DEPTH, N_EXPERTS, D_FF, D_MODEL), D_FF ** -0.5),
        "b_down": nrm(ks[20], (DEPTH, N_EXPERTS, D_MODEL), 0.01),
        "g_final": 1.0 + nrm(ks[21], (D_MODEL,), 0.02),
    }


def reference(x_prompt, x_sample, cache_k, cache_v, state_conv, g_attn_norm, w_in, lambda_q,
              lambda_k, g_subln, conv_w, w_attn_proj, w_conv_proj, w_out, g_ffn_norm, w_router,
              b_router, w_gu, b_gu, w_down, b_down, g_final):
    hp, hs = x_prompt, x_sample
    kp_l, vp_l, cp_l, ks_l, vs_l, cs_l = [], [], [], [], [], []
    for layer in range(DEPTH):
        zero_conv = jnp.zeros((hp.shape[0], CONV_W - 1, D_CONV), hp.dtype)
        mix_p, kp, vp, cp = _mixer(_rmsnorm(hp, g_attn_norm[layer]), zero_conv, None, None, layer,
                                   w_in, lambda_q, lambda_k, g_subln, conv_w,
                                   w_attn_proj, w_conv_proj, w_out)
        hp = hp + mix_p
        hp = hp + _moe(_rmsnorm(hp, g_ffn_norm[layer]), layer, w_router, b_router, w_gu, b_gu, w_down, b_down)
        mix_s, kq, vq, cq = _mixer(_rmsnorm(hs, g_attn_norm[layer]), state_conv[layer], cache_k[layer],
                                   cache_v[layer], layer, w_in, lambda_q, lambda_k, g_subln, conv_w,
                                   w_attn_proj, w_conv_proj, w_out)
        hs = hs + mix_s
        hs = hs + _moe(_rmsnorm(hs, g_ffn_norm[layer]), layer, w_router, b_router, w_gu, b_gu, w_down, b_down)
        kp_l.append(kp); vp_l.append(vp); cp_l.append(cp)
        ks_l.append(kq); vs_l.append(vq); cs_l.append(cq)
    y_prompt = _rmsnorm(hp, g_final)
    y_sample = _rmsnorm(hs, g_final)
    k_prompt = jnp.stack(kp_l, axis=0)
    v_prompt = jnp.stack(vp_l, axis=0)
    conv_prompt = jnp.stack(cp_l, axis=0)
    k_sample = jnp.stack(ks_l, axis=0)
    v_sample = jnp.stack(vs_l, axis=0)
    conv_sample = jnp.stack(cs_l, axis=0)
    return (y_prompt, y_sample, k_prompt, v_prompt, conv_prompt, k_sample, v_sample, conv_sample)
```

```python
import functools
import math

import jax
import jax.numpy as jnp
from jax import lax
from jax.experimental import pallas as pl
from jax.experimental.pallas import tpu as pltpu

N_HEADS = 8
HEAD_DIM = 64
V_DIM = 2 * HEAD_DIM
CHUNK = 64
N_EXPERTS = 32
TOP_K = 4
SWIGLU_LIMIT = 7.0
SWIGLU_ALPHA = 1.702
EPS = 1e-6
NEG = -1e30
LAMBDA_INIT = 0.8 - 0.6 * math.exp(-0.3 * 0)
N_SEG = 8

ROW_TILE = 256
ATTN_TILE = 256
MOE_TILE = 256
MIB = 1024 * 1024

BF16 = jnp.bfloat16
F32 = jnp.float32


def _cparams(semantics, vmem_mib):
    return pltpu.CompilerParams(dimension_semantics=semantics, vmem_limit_bytes=vmem_mib * MIB)


def _const_spec(shape):
    nd = len(shape)
    return pl.BlockSpec(shape, lambda *_: (0,) * nd, pipeline_mode=pl.Buffered(1))


def _sigmoid(x):
    return 1.0 / (1.0 + jnp.exp(-x))


def _dot(a, b):
    return jnp.dot(a, b, preferred_element_type=F32)


def _dot_nt(a, b):
    return lax.dot_general(a, b, (((1,), (1,)), ((), ())), preferred_element_type=F32)


def _inproj_kernel(x_ref, prev_ref, g_ref, w_ref, cw_ref, wcp_ref,
                   q_ref, k32_ref, v32_ref, kb_ref, vb_ref, sga_ref, gcc_ref, cnew_ref, ubuf):
    t = pl.program_id(1)
    tm, d = x_ref.shape[1], x_ref.shape[2]

    @pl.when(t == 0)
    def _():
        ubuf[0:8, :] = prev_ref[0]

    x = x_ref[0]
    ms = jnp.mean(x * x, axis=-1, keepdims=True)
    xn = ((x * lax.rsqrt(ms + EPS)) * g_ref[...]).astype(BF16)

    def seg(i):
        return _dot(xn, w_ref[:, i * d:(i + 1) * d])

    q_ref[0] = (seg(0) * (HEAD_DIM ** -0.5)).astype(BF16)
    k = seg(1)
    k32_ref[0] = k
    kb_ref[0] = k.astype(BF16)
    v = seg(2)
    v32_ref[0] = v
    vb_ref[0] = v.astype(BF16)

    u = seg(5) * seg(3)
    ubuf[8:8 + tm, :] = u
    conv = (ubuf[6:6 + tm, :] * cw_ref[0:1, :] + ubuf[7:7 + tm, :] * cw_ref[1:2, :]
            + u * cw_ref[2:3, :])
    cpre = (seg(4) * conv).astype(BF16)
    c_out = _dot(cpre, wcp_ref[...])
    gcc_ref[0] = (_sigmoid(seg(7)) * c_out).astype(BF16)
    sga_ref[0] = _sigmoid(seg(6)).astype(BF16)
    cnew_ref[0] = ubuf[tm + 6:tm + 8, :]
    ubuf[0:8, :] = ubuf[tm:tm + 8, :]


def _inproj(x, conv_prev8, g, w_in_b, conv_w, wcp_b):
    b, t, d = x.shape
    tm = min(t, ROW_TILE)
    assert t % tm == 0 and tm % 8 == 0
    row = lambda: pl.BlockSpec((1, tm, d), lambda i, j: (i, j, 0))
    bf = jax.ShapeDtypeStruct((b, t, d), BF16)
    f32 = jax.ShapeDtypeStruct((b, t, d), F32)
    return pl.pallas_call(
        _inproj_kernel,
        grid=(b, t // tm),
        in_specs=[row(),
                  pl.BlockSpec((1, 8, d), lambda i, j: (i, 0, 0)),
                  _const_spec((1, d)),
                  _const_spec((d, N_SEG * d)),
                  _const_spec((3, d)),
                  _const_spec((d, d))],
        out_specs=[row(), row(), row(), row(), row(), row(), row(),
                   pl.BlockSpec((1, 2, d), lambda i, j: (i, 0, 0))],
        out_shape=[bf, f32, f32, bf, bf, bf, bf, jax.ShapeDtypeStruct((b, 2, d), F32)],
        scratch_shapes=[pltpu.VMEM((tm + 8, d), F32)],
        compiler_params=_cparams(("arbitrary", "arbitrary"), 56),
        name="inproj",
    )(x, conv_prev8, g, w_in_b, conv_w, wcp_b)


def _lambda(lq_ref, lk_ref):
    e = jnp.exp(jnp.sum(lq_ref[...] * lk_ref[...], axis=-1, keepdims=True))
    return e[0:1, :] - e[1:2, :] + LAMBDA_INIT


def _stack_maps(q):
    lane = lax.broadcasted_iota(jnp.int32, q.shape, 1)
    zero = jnp.zeros_like(q)
    return jnp.concatenate([jnp.where(lane < HEAD_DIM, q, zero),
                            jnp.where(lane >= HEAD_DIM, q, zero)], axis=0)


def _attn_finish(acc, l, lam, gs, tq):
    o = acc / l
    o = o[:tq] - lam * o[tq:]
    ms = jnp.mean(o * o, axis=-1, keepdims=True)
    return ((o * lax.rsqrt(ms + EPS)) * gs) * (1.0 - LAMBDA_INIT)


def _attn_prompt_kernel(lq_ref, lk_ref, gs_ref, q_ref, k_ref, v_ref, o_ref, *, tile):
    i = pl.program_id(2)
    qq = _stack_maps(q_ref[0])

    def step(j, carry, masked):
        m, l, acc = carry
        kj = k_ref[0, pl.ds(pl.multiple_of(j * tile, tile), tile), :]
        vj = v_ref[0, pl.ds(pl.multiple_of(j * tile, tile), tile), :]
        s = _dot_nt(qq, kj)
        if masked:
            r = lax.broadcasted_iota(jnp.int32, s.shape, 0)
            c = lax.broadcasted_iota(jnp.int32, s.shape, 1)
            r = jnp.where(r >= tile, r - tile, r)
            s = jnp.where((c // CHUNK) <= (r // CHUNK), s, NEG)
        m_new = jnp.maximum(m, jnp.max(s, axis=-1, keepdims=True))
        alpha = jnp.exp(m - m_new)
        p = jnp.exp(s - m_new)
        l = alpha * l + jnp.sum(p, axis=-1, keepdims=True)
        acc = alpha * acc + _dot(p.astype(BF16), vj)
        return m_new, l, acc

    init = (jnp.full((2 * tile, 1), NEG, F32), jnp.zeros((2 * tile, 1), F32),
            jnp.zeros((2 * tile, V_DIM), F32))
    carry = lax.fori_loop(0, i, lambda j, c: step(j, c, False), init)
    _, l, acc = step(i, carry, True)
    o_ref[0] = _attn_finish(acc, l, _lambda(lq_ref, lk_ref), gs_ref[...], tile).astype(BF16)


def _attn_prompt(lq, lk, gs, q, kb, vb):
    b, t, d = q.shape
    tile = min(t, ATTN_TILE)
    assert t % tile == 0 and tile % CHUNK == 0
    qspec = pl.BlockSpec((1, tile, V_DIM), lambda bi, h, i: (bi, i, h))
    kspec = pl.BlockSpec((1, t, V_DIM), lambda bi, h, i: (bi, 0, h))
    return pl.pallas_call(
        functools.partial(_attn_prompt_kernel, tile=tile),
        grid=(b, N_HEADS, t // tile),
        in_specs=[_const_spec((2, HEAD_DIM)), _const_spec((2, HEAD_DIM)), _const_spec((1, V_DIM)),
                  qspec, kspec, kspec],
        out_specs=qspec,
        out_shape=jax.ShapeDtypeStruct((b, t, d), BF16),
        compiler_params=_cparams(("arbitrary", "arbitrary", "arbitrary"), 40),
        name="attn_prompt",
    )(lq, lk, gs, q, kb, vb)


def _attn_sample_kernel(lq_ref, lk_ref, gs_ref, q_ref, ck_ref, cv_ref, kn_ref, vn_ref, o_ref):
    tq = q_ref.shape[1]
    qq = _stack_maps(q_ref[0])
    s_p = _dot_nt(qq, ck_ref[0].astype(BF16))
    s_n = _dot_nt(qq, kn_ref[0])
    m = jnp.maximum(jnp.max(s_p, axis=-1, keepdims=True), jnp.max(s_n, axis=-1, keepdims=True))
    p_p = jnp.exp(s_p - m)
    p_n = jnp.exp(s_n - m)
    l = jnp.sum(p_p, axis=-1, keepdims=True) + jnp.sum(p_n, axis=-1, keepdims=True)
    acc = _dot(p_p.astype(BF16), cv_ref[0].astype(BF16)) + _dot(p_n.astype(BF16), vn_ref[0])
    o_ref[0] = _attn_finish(acc, l, _lambda(lq_ref, lk_ref), gs_ref[...], tq).astype(BF16)


def _attn_sample(lq, lk, gs, q, cache_k, cache_v, kb, vb):
    b, t, d = q.shape
    past = cache_k.shape[1]
    qspec = pl.BlockSpec((1, t, V_DIM), lambda bi, h: (bi, 0, h))
    cspec = pl.BlockSpec((1, past, V_DIM), lambda bi, h: (bi, 0, h))
    return pl.pallas_call(
        _attn_sample_kernel,
        grid=(b, N_HEADS),
        in_specs=[_const_spec((2, HEAD_DIM)), _const_spec((2, HEAD_DIM)), _const_spec((1, V_DIM)),
                  qspec, cspec, cspec, qspec, qspec],
        out_specs=qspec,
        out_shape=jax.ShapeDtypeStruct((b, t, d), BF16),
        compiler_params=_cparams(("arbitrary", "arbitrary"), 32),
        name="attn_sample",
    )(lq, lk, gs, q, cache_k, cache_v, kb, vb)


def _cols4(cols, dtype):
    tm = cols[0].shape[0]
    lane = lax.broadcasted_iota(jnp.int32, (tm, TOP_K), 1)
    out = jnp.broadcast_to(cols[3], (tm, TOP_K))
    for kk in (2, 1, 0):
        out = jnp.where(lane == kk, cols[kk], out)
    return out.astype(dtype)


def _post_kernel(cnt_in_ref, x_ref, on_ref, sga_ref, gcc_ref, wap_ref, wout_ref, g_ref, wr_ref, br_ref,
                 h_ref, hn_ref, ti_ref, tw_ref, rk_ref, cnt_ref, carry):
    step = pl.program_id(0)
    tm = x_ref.shape[0]

    @pl.when(step == 0)
    def _():
        carry[...] = cnt_in_ref[...]

    a_out = _dot(on_ref[...], wap_ref[...])
    merged = sga_ref[...].astype(F32) * a_out + gcc_ref[...].astype(F32)
    h = x_ref[...] + _dot(merged.astype(BF16), wout_ref[...])
    h_ref[...] = h
    ms = jnp.mean(h * h, axis=-1, keepdims=True)
    hn = ((h * lax.rsqrt(ms + EPS)) * g_ref[...]).astype(BF16)
    hn_ref[...] = hn

    logits = _dot(hn, wr_ref[...]) + br_ref[...]
    lane = lax.broadcasted_iota(jnp.int32, logits.shape, 1)
    work = logits
    vals, idxs, sels = [], [], []
    for _ in range(TOP_K):
        mk = jnp.max(work, axis=-1, keepdims=True)
        ik = jnp.min(jnp.where(work == mk, lane, N_EXPERTS), axis=-1, keepdims=True)
        sel = lane == ik
        work = jnp.where(sel, -jnp.inf, work)
        vals.append(mk)
        idxs.append(ik)
        sels.append(sel)
    es = [jnp.exp(vk - vals[0]) for vk in vals]
    denom = es[0] + es[1] + es[2] + es[3]
    ti_ref[...] = _cols4(idxs, jnp.int32)
    tw_ref[...] = _cols4([e / denom for e in es], F32)

    picked = (sels[0] | sels[1] | sels[2] | sels[3])
    r = lax.broadcasted_iota(jnp.int32, (tm, tm), 0)
    c = lax.broadcasted_iota(jnp.int32, (tm, tm), 1)
    before = _dot((c < r).astype(BF16), picked.astype(BF16)) + carry[...]
    ranks = [jnp.sum(jnp.where(sk, before, 0.0), axis=-1, keepdims=True) for sk in sels]
    rk_ref[...] = _cols4(ranks, jnp.int32)
    carry[...] = carry[...] + jnp.sum(picked.astype(F32), axis=0, keepdims=True)
    cnt_ref[...] = carry[...]


def _post(cnt_in, x, on, sga, gcc, wap_b, wout_b, g, wr_b, br):
    n, d = x.shape
    tm = min(n, ROW_TILE)
    assert n % tm == 0
    row = lambda: pl.BlockSpec((tm, d), lambda i: (i, 0))
    k4 = lambda: pl.BlockSpec((tm, TOP_K), lambda i: (i, 0))
    cnt = lambda: pl.BlockSpec((1, N_EXPERTS), lambda i: (0, 0))
    return pl.pallas_call(
        _post_kernel,
        grid=(n // tm,),
        in_specs=[cnt(), row(), row(), row(), row(),
                  _const_spec((d, d)), _const_spec((d, d)), _const_spec((1, d)),
                  _const_spec((d, N_EXPERTS)), _const_spec((1, N_EXPERTS))],
        out_specs=[row(), row(), k4(), k4(), k4(), cnt()],
        out_shape=[jax.ShapeDtypeStruct((n, d), F32), jax.ShapeDtypeStruct((n, d), BF16),
                   jax.ShapeDtypeStruct((n, TOP_K), jnp.int32), jax.ShapeDtypeStruct((n, TOP_K), F32),
                   jax.ShapeDtypeStruct((n, TOP_K), jnp.int32),
                   jax.ShapeDtypeStruct((1, N_EXPERTS), F32)],
        scratch_shapes=[pltpu.VMEM((1, N_EXPERTS), F32)],
        compiler_params=_cparams(("arbitrary",), 32),
        name="post",
    )(cnt_in, x, on, sga, gcc, wap_b, wout_b, g, wr_b, br)


def _moe_kernel(te_ref, tv_ref, x_ref, wg_ref, wu_ref, bg_ref, bu_ref, wd_ref, bd_ref, y_ref):
    @pl.when(tv_ref[pl.program_id(0)] > 0)
    def _():
        x = x_ref[...]
        gate = jnp.minimum(_dot(x, wg_ref[0]) + bg_ref[0], SWIGLU_LIMIT)
        up = jnp.clip(_dot(x, wu_ref[0]) + bu_ref[0], -SWIGLU_LIMIT, SWIGLU_LIMIT)
        glu = gate * _sigmoid(SWIGLU_ALPHA * gate)
        y = _dot((glu * (up + 1.0)).astype(BF16), wd_ref[0]) + bd_ref[0]
        y_ref[...] = y.astype(BF16)


def _moe(tile_expert, tile_valid, xs, wg_b, wu_b, bg, bu, wd_b, bd):
    p, d = xs.shape
    f = wg_b.shape[2]
    n_tiles = p // MOE_TILE
    row = pl.BlockSpec((MOE_TILE, d), lambda i, te, tv: (i, 0))
    wspec = lambda a, b_: pl.BlockSpec((1, a, b_), lambda i, te, tv: (te[i], 0, 0))
    return pl.pallas_call(
        _moe_kernel,
        grid_spec=pltpu.PrefetchScalarGridSpec(
            num_scalar_prefetch=2,
            grid=(n_tiles,),
            in_specs=[row, wspec(d, f), wspec(d, f), wspec(1, f), wspec(1, f), wspec(f, d), wspec(1, d)],
            out_specs=row),
        out_shape=jax.ShapeDtypeStruct((p, d), BF16),
        compiler_params=_cparams(("arbitrary",), 40),
        name="moe",
    )(tile_expert, tile_valid, xs, wg_b, wu_b, bg, bu, wd_b, bd)


def _final_kernel(h_ref, yg_ref, tw_ref, g_ref, o_ref):
    tw = tw_ref[...]
    acc = yg_ref[0].astype(F32) * tw[:, 0:1]
    for kk in range(1, TOP_K):
        acc = acc + yg_ref[kk].astype(F32) * tw[:, kk:kk + 1]
    h = h_ref[...] + acc
    ms = jnp.mean(h * h, axis=-1, keepdims=True)
    o_ref[...] = (h * lax.rsqrt(ms + EPS)) * g_ref[...]


def _final(h, yg, tw, g):
    n, d = h.shape
    tm = min(n, ROW_TILE)
    return pl.pallas_call(
        _final_kernel,
        grid=(n // tm,),
        in_specs=[pl.BlockSpec((tm, d), lambda i: (i, 0)),
                  pl.BlockSpec((TOP_K, tm, d), lambda i: (0, i, 0)),
                  pl.BlockSpec((tm, TOP_K), lambda i: (i, 0)),
                  _const_spec((1, d))],
        out_specs=pl.BlockSpec((tm, d), lambda i: (i, 0)),
        out_shape=jax.ShapeDtypeStruct((n, d), F32),
        compiler_params=_cparams(("arbitrary",), 32),
        name="final",
    )(h, yg, tw, g)


def kernel(x_prompt, x_sample, cache_k, cache_v, state_conv, g_attn_norm, w_in, lambda_q, lambda_k,
           g_subln, conv_w, w_attn_proj, w_conv_proj, w_out, g_ffn_norm, w_router, b_router, w_gu, b_gu,
           w_down, b_down, g_final):
    assert w_in.shape[0] == 1, "single-layer trunk"
    bp, tp, d = x_prompt.shape
    bs, ts, _ = x_sample.shape
    past = cache_k.shape[2]
    f = w_down.shape[2]
    n_p, n_s = bp * tp, bs * ts
    n_all = n_p + n_s

    w_in_b = w_in[0].astype(BF16)
    wcp_b = w_conv_proj[0].astype(BF16)
    wap_b = w_attn_proj[0].astype(BF16)
    wout_b = w_out[0].astype(BF16)
    wr_b = w_router[0].astype(BF16)
    wgu = w_gu[0].reshape(N_EXPERTS, d, f, 2)
    wg_b = wgu[..., 0].astype(BF16)
    wu_b = wgu[..., 1].astype(BF16)
    bgu = b_gu[0].reshape(N_EXPERTS, 1, f, 2)
    bg, bu = bgu[..., 0], bgu[..., 1]
    wd_b = w_down[0].astype(BF16)
    bd = b_down[0].reshape(N_EXPERTS, 1, d)
    g_attn = g_attn_norm[0].reshape(1, d)
    g_ffn = g_ffn_norm[0].reshape(1, d)
    gs = g_subln[0].reshape(1, V_DIM)
    lq, lk = lambda_q[0], lambda_k[0]
    br = b_router[0].reshape(1, N_EXPERTS)

    prev_p = jnp.zeros((bp, 8, d), F32)
    prev_s = jnp.pad(state_conv[0], ((0, 0), (6, 0), (0, 0)))

    q_p, k_p, v_p, kb_p, vb_p, sga_p, gcc_p, conv_p = _inproj(x_prompt, prev_p, g_attn, w_in_b, conv_w[0], wcp_b)
    q_s, k_s, v_s, kb_s, vb_s, sga_s, gcc_s, conv_s = _inproj(x_sample, prev_s, g_attn, w_in_b, conv_w[0], wcp_b)

    on_p = _attn_prompt(lq, lk, gs, q_p, kb_p, vb_p)
    on_s = _attn_sample(lq, lk, gs, q_s, cache_k[0].reshape(bs, past, d), cache_v[0].reshape(bs, past, d),
                        kb_s, vb_s)

    flat = lambda a, n: a.reshape(n, d)
    cnt0 = jnp.zeros((1, N_EXPERTS), F32)
    h_p, hn_p, ti_p, tw_p, rk_p, cnt_p = _post(cnt0, flat(x_prompt, n_p), flat(on_p, n_p), flat(sga_p, n_p),
                                               flat(gcc_p, n_p), wap_b, wout_b, g_ffn, wr_b, br)
    h_s, hn_s, ti_s, tw_s, rk_s, cnt = _post(cnt_p, flat(x_sample, n_s), flat(on_s, n_s), flat(sga_s, n_s),
                                             flat(gcc_s, n_s), wap_b, wout_b, g_ffn, wr_b, br)

    counts = cnt[0].astype(jnp.int32)
    padded = ((counts + MOE_TILE - 1) // MOE_TILE) * MOE_TILE
    ends = jnp.cumsum(padded)
    starts = ends - padded
    n_tiles = (n_all * TOP_K) // MOE_TILE + N_EXPERTS
    tile_row = jnp.arange(n_tiles, dtype=jnp.int32) * MOE_TILE
    tile_expert = jnp.minimum(jnp.searchsorted(ends, tile_row, side="right"), N_EXPERTS - 1).astype(jnp.int32)
    tile_valid = (tile_row < ends[-1]).astype(jnp.int32)

    top_i = jnp.concatenate([ti_p, ti_s], axis=0)
    rank = jnp.concatenate([rk_p, rk_s], axis=0)
    pos = starts[top_i] + rank
    hn_all = jnp.concatenate([hn_p, hn_s], axis=0)

    token = jnp.broadcast_to(jnp.arange(n_all, dtype=jnp.int32)[:, None], (n_all, TOP_K))
    src = jnp.zeros((n_tiles * MOE_TILE,), jnp.int32).at[pos.reshape(-1)].set(token.reshape(-1))
    xs = jnp.take(hn_all, src, axis=0)

    ys = _moe(tile_expert, tile_valid, xs, wg_b, wu_b, bg, bu, wd_b, bd)
    yg = jnp.take(ys, pos.T, axis=0)

    y_p = _final(h_p, yg[:, :n_p], tw_p, g_final.reshape(1, d))
    y_s = _final(h_s, yg[:, n_p:], tw_s, g_final.reshape(1, d))

    hd = N_HEADS
    return (y_p.reshape(bp, tp, d), y_s.reshape(bs, ts, d),
            k_p.reshape(1, bp, tp, hd, V_DIM), v_p.reshape(1, bp, tp, hd, V_DIM), conv_p[None],
            k_s.reshape(1, bs, ts, hd, V_DIM), v_s.reshape(1, bs, ts, hd, V_DIM), conv_s[None])
```

```python
import functools
import math

import jax
import jax.numpy as jnp
from jax import lax
from jax.experimental import pallas as pl
from jax.experimental.pallas import tpu as pltpu

N_HEADS = 8
HEAD_DIM = 64
V_DIM = 2 * HEAD_DIM
CHUNK = 64
N_EXPERTS = 32
TOP_K = 4
SWIGLU_LIMIT = 7.0
SWIGLU_ALPHA = 1.702
EPS = 1e-6
NEG = -1e30
LAMBDA_INIT = 0.8 - 0.6 * math.exp(-0.3 * 0)
N_SEG = 8
Q_SCALE = HEAD_DIM ** -0.5 * math.log2(math.e)

ROW_TILE = 256
ATTN_TILE = 512
MOE_TILE = 256
MIB = 1024 * 1024

BF16 = jnp.bfloat16
F32 = jnp.float32


def _cparams(semantics, vmem_mib):
    return pltpu.CompilerParams(dimension_semantics=semantics, vmem_limit_bytes=vmem_mib * MIB)


def _const_spec(shape):
    nd = len(shape)
    return pl.BlockSpec(shape, lambda *_: (0,) * nd, pipeline_mode=pl.Buffered(1))


def _sigmoid(x):
    return 1.0 / (1.0 + jnp.exp(-x))


def _dot(a, b):
    return jnp.dot(a, b, preferred_element_type=F32)


def _dot_nt(a, b):
    return lax.dot_general(a, b, (((1,), (1,)), ((), ())), preferred_element_type=F32)


def _inproj_kernel(x_ref, prev_ref, g_ref, w_ref, cw_ref, wcp_ref,
                   q_ref, k32_ref, v32_ref, kb_ref, vb_ref, sga_ref, gcc_ref, cnew_ref, ubuf):
    t = pl.program_id(1)
    tm, d = x_ref.shape[1], x_ref.shape[2]

    @pl.when(t == 0)
    def _():
        ubuf[0:8, :] = prev_ref[0]

    x = x_ref[0]
    ms = jnp.mean(x * x, axis=-1, keepdims=True)
    xn = ((x * lax.rsqrt(ms + EPS)) * g_ref[...]).astype(BF16)

    def seg(i):
        return _dot(xn, w_ref[:, i * d:(i + 1) * d])

    q_ref[0] = (seg(0) * Q_SCALE).astype(BF16)
    k = seg(1)
    k32_ref[0] = k
    kb_ref[0] = k.astype(BF16)
    v = seg(2)
    v32_ref[0] = v
    vb_ref[0] = v.astype(BF16)

    u = seg(5) * seg(3)
    ubuf[8:8 + tm, :] = u
    conv = (ubuf[6:6 + tm, :] * cw_ref[0:1, :] + ubuf[7:7 + tm, :] * cw_ref[1:2, :]
            + u * cw_ref[2:3, :])
    cpre = (seg(4) * conv).astype(BF16)
    c_out = _dot(cpre, wcp_ref[...])
    gcc_ref[0] = (_sigmoid(seg(7)) * c_out).astype(BF16)
    sga_ref[0] = _sigmoid(seg(6)).astype(BF16)
    cnew_ref[0] = ubuf[tm + 6:tm + 8, :]
    ubuf[0:8, :] = ubuf[tm:tm + 8, :]


def _inproj(x, conv_prev8, g, w_in_b, conv_w, wcp_b):
    b, t, d = x.shape
    tm = min(t, ROW_TILE)
    assert t % tm == 0 and tm % 8 == 0
    row = lambda: pl.BlockSpec((1, tm, d), lambda i, j: (i, j, 0))
    bf = jax.ShapeDtypeStruct((b, t, d), BF16)
    f32 = jax.ShapeDtypeStruct((b, t, d), F32)
    return pl.pallas_call(
        _inproj_kernel,
        grid=(b, t // tm),
        in_specs=[row(),
                  pl.BlockSpec((1, 8, d), lambda i, j: (i, 0, 0)),
                  _const_spec((1, d)),
                  _const_spec((d, N_SEG * d)),
                  _const_spec((3, d)),
                  _const_spec((d, d))],
        out_specs=[row(), row(), row(), row(), row(), row(), row(),
                   pl.BlockSpec((1, 2, d), lambda i, j: (i, 0, 0))],
        out_shape=[bf, f32, f32, bf, bf, bf, bf, jax.ShapeDtypeStruct((b, 2, d), F32)],
        scratch_shapes=[pltpu.VMEM((tm + 8, d), F32)],
        compiler_params=_cparams(("arbitrary", "arbitrary"), 56),
        name="inproj",
    )(x, conv_prev8, g, w_in_b, conv_w, wcp_b)


def _lambda(lq_ref, lk_ref):
    e = jnp.exp(jnp.sum(lq_ref[...] * lk_ref[...], axis=-1, keepdims=True))
    return e[0:1, :] - e[1:2, :] + LAMBDA_INIT


def _stack_maps(q):
    lane = lax.broadcasted_iota(jnp.int32, q.shape, 1)
    zero = jnp.zeros_like(q)
    return jnp.concatenate([jnp.where(lane < HEAD_DIM, q, zero),
                            jnp.where(lane >= HEAD_DIM, q, zero)], axis=0)


def _attn_finish(acc, l, lam, gs, tq):
    o = acc / l
    o = o[:tq] - lam * o[tq:]
    ms = jnp.mean(o * o, axis=-1, keepdims=True)
    return ((o * lax.rsqrt(ms + EPS)) * gs) * (1.0 - LAMBDA_INIT)


def _attn_prompt_kernel(lq_ref, lk_ref, gs_ref, q_ref, k_ref, v_ref, o_ref,
                        s_sc, p_sc, m_sc, l_sc, acc_sc, *, tile):
    i = pl.program_id(2)
    qq = _stack_maps(q_ref[0])
    lane_tiles = [slice(c * V_DIM, (c + 1) * V_DIM) for c in range(tile // V_DIM)]

    def kv_tile(ref, j):
        return ref[0, pl.ds(pl.multiple_of(j * tile, tile), tile), :]

    def scores(j, masked):
        s = _dot_nt(qq, kv_tile(k_ref, j))
        if masked:
            r = lax.broadcasted_iota(jnp.int32, s.shape, 0)
            c = lax.broadcasted_iota(jnp.int32, s.shape, 1)
            r = jnp.where(r >= tile, r - tile, r)
            s = jnp.where((c // CHUNK) <= (r // CHUNK), s, NEG)
        s_sc[...] = s

    def prev_pv(j):
        return _dot(p_sc[...], kv_tile(v_ref, j))

    def softmax(pv):
        m_cur = s_sc[:, lane_tiles[0]]
        for lt in lane_tiles[1:]:
            m_cur = jnp.maximum(m_cur, s_sc[:, lt])
        m_old = m_sc[...]
        m_new = jnp.maximum(m_old, jnp.max(m_cur, axis=-1, keepdims=True))
        alpha = jnp.exp2(m_old - m_new)
        psum = None
        for lt in lane_tiles:
            pc = jnp.exp2(s_sc[:, lt] - m_new)
            p_sc[:, lt] = pc.astype(BF16)
            psum = pc if psum is None else psum + pc
        l_sc[...] = alpha * l_sc[...] + psum
        acc_sc[...] = alpha * (acc_sc[...] + pv)
        m_sc[...] = m_new

    def tile_of(k):
        return jnp.where(k == 0, i, k - 1)

    m_sc[...] = jnp.full(m_sc.shape, NEG, F32)
    l_sc[...] = jnp.zeros(l_sc.shape, F32)
    acc_sc[...] = jnp.zeros(acc_sc.shape, F32)
    p_sc[...] = jnp.zeros(p_sc.shape, BF16)
    scores(i, True)

    def body(k, carry):
        pv = prev_pv(tile_of(jnp.maximum(k - 1, 0)))
        softmax(pv)
        scores(jnp.minimum(k, jnp.maximum(i - 1, 0)), False)
        return carry

    lax.fori_loop(0, i + 1, body, 0)
    acc = acc_sc[...] + prev_pv(tile_of(i))
    l = jnp.sum(l_sc[...], axis=-1, keepdims=True)
    o_ref[0] = _attn_finish(acc, l, _lambda(lq_ref, lk_ref), gs_ref[...], tile).astype(BF16)


def _attn_prompt(lq, lk, gs, q, kb, vb):
    b, t, d = q.shape
    tile = min(t, ATTN_TILE)
    assert t % tile == 0 and tile % CHUNK == 0 and tile % V_DIM == 0
    qspec = pl.BlockSpec((1, tile, V_DIM), lambda bi, h, i: (bi, i, h))
    kspec = pl.BlockSpec((1, t, V_DIM), lambda bi, h, i: (bi, 0, h))
    rows = 2 * tile
    return pl.pallas_call(
        functools.partial(_attn_prompt_kernel, tile=tile),
        grid=(b, N_HEADS, t // tile),
        in_specs=[_const_spec((2, HEAD_DIM)), _const_spec((2, HEAD_DIM)), _const_spec((1, V_DIM)),
                  qspec, kspec, kspec],
        out_specs=qspec,
        out_shape=jax.ShapeDtypeStruct((b, t, d), BF16),
        scratch_shapes=[pltpu.VMEM((rows, tile), F32), pltpu.VMEM((rows, tile), BF16),
                        pltpu.VMEM((rows, V_DIM), F32), pltpu.VMEM((rows, V_DIM), F32),
                        pltpu.VMEM((rows, V_DIM), F32)],
        compiler_params=_cparams(("arbitrary", "arbitrary", "arbitrary"), 40),
        name="attn_prompt",
    )(lq, lk, gs, q, kb, vb)


def _attn_sample_kernel(lq_ref, lk_ref, gs_ref, q_ref, ck_ref, cv_ref, kn_ref, vn_ref, o_ref):
    tq = q_ref.shape[1]
    qq = _stack_maps(q_ref[0])
    s_p = _dot_nt(qq, ck_ref[0].astype(BF16))
    s_n = _dot_nt(qq, kn_ref[0])
    m = jnp.maximum(jnp.max(s_p, axis=-1, keepdims=True), jnp.max(s_n, axis=-1, keepdims=True))
    p_p = jnp.exp2(s_p - m)
    p_n = jnp.exp2(s_n - m)
    l = jnp.sum(p_p, axis=-1, keepdims=True) + jnp.sum(p_n, axis=-1, keepdims=True)
    acc = _dot(p_p.astype(BF16), cv_ref[0].astype(BF16)) + _dot(p_n.astype(BF16), vn_ref[0])
    o_ref[0] = _attn_finish(acc, l, _lambda(lq_ref, lk_ref), gs_ref[...], tq).astype(BF16)


def _attn_sample(lq, lk, gs, q, cache_k, cache_v, kb, vb):
    b, t, d = q.shape
    past = cache_k.shape[1]
    qspec = pl.BlockSpec((1, t, V_DIM), lambda bi, h: (bi, 0, h))
    cspec = pl.BlockSpec((1, past, V_DIM), lambda bi, h: (bi, 0, h))
    return pl.pallas_call(
        _attn_sample_kernel,
        grid=(b, N_HEADS),
        in_specs=[_const_spec((2, HEAD_DIM)), _const_spec((2, HEAD_DIM)), _const_spec((1, V_DIM)),
                  qspec, cspec, cspec, qspec, qspec],
        out_specs=qspec,
        out_shape=jax.ShapeDtypeStruct((b, t, d), BF16),
        compiler_params=_cparams(("arbitrary", "arbitrary"), 32),
        name="attn_sample",
    )(lq, lk, gs, q, cache_k, cache_v, kb, vb)


def _cols4(cols, dtype):
    tm = cols[0].shape[0]
    lane = lax.broadcasted_iota(jnp.int32, (tm, TOP_K), 1)
    out = jnp.broadcast_to(cols[3], (tm, TOP_K))
    for kk in (2, 1, 0):
        out = jnp.where(lane == kk, cols[kk], out)
    return out.astype(dtype)


def _post_kernel(cnt_in_ref, x_ref, on_ref, sga_ref, gcc_ref, wap_ref, wout_ref, g_ref, wr_ref, br_ref,
                 h_ref, hn_ref, ti_ref, tw_ref, rk_ref, cnt_ref, carry):
    step = pl.program_id(0)
    tm = x_ref.shape[0]

    @pl.when(step == 0)
    def _():
        carry[...] = cnt_in_ref[...]

    a_out = _dot(on_ref[...], wap_ref[...])
    merged = sga_ref[...].astype(F32) * a_out + gcc_ref[...].astype(F32)
    h = x_ref[...] + _dot(merged.astype(BF16), wout_ref[...])
    h_ref[...] = h
    ms = jnp.mean(h * h, axis=-1, keepdims=True)
    hn = ((h * lax.rsqrt(ms + EPS)) * g_ref[...]).astype(BF16)
    hn_ref[...] = hn

    logits = _dot(hn, wr_ref[...]) + br_ref[...]
    lane = lax.broadcasted_iota(jnp.int32, logits.shape, 1)
    work = logits
    vals, idxs, sels = [], [], []
    for _ in range(TOP_K):
        mk = jnp.max(work, axis=-1, keepdims=True)
        ik = jnp.min(jnp.where(work == mk, lane, N_EXPERTS), axis=-1, keepdims=True)
        sel = lane == ik
        work = jnp.where(sel, -jnp.inf, work)
        vals.append(mk)
        idxs.append(ik)
        sels.append(sel)
    es = [jnp.exp(vk - vals[0]) for vk in vals]
    denom = es[0] + es[1] + es[2] + es[3]
    ti_ref[...] = _cols4(idxs, jnp.int32)
    tw_ref[...] = _cols4([e / denom for e in es], F32)

    picked = (sels[0] | sels[1] | sels[2] | sels[3])
    r = lax.broadcasted_iota(jnp.int32, (tm, tm), 0)
    c = lax.broadcasted_iota(jnp.int32, (tm, tm), 1)
    before = _dot((c < r).astype(BF16), picked.astype(BF16)) + carry[...]
    ranks = [jnp.sum(jnp.where(sk, before, 0.0), axis=-1, keepdims=True) for sk in sels]
    rk_ref[...] = _cols4(ranks, jnp.int32)
    carry[...] = carry[...] + jnp.sum(picked.astype(F32), axis=0, keepdims=True)
    cnt_ref[...] = carry[...]


def _post(cnt_in, x, on, sga, gcc, wap_b, wout_b, g, wr_b, br):
    n, d = x.shape
    tm = min(n, ROW_TILE)
    assert n % tm == 0
    row = lambda: pl.BlockSpec((tm, d), lambda i: (i, 0))
    k4 = lambda: pl.BlockSpec((tm, TOP_K), lambda i: (i, 0))
    cnt = lambda: pl.BlockSpec((1, N_EXPERTS), lambda i: (0, 0))
    return pl.pallas_call(
        _post_kernel,
        grid=(n // tm,),
        in_specs=[cnt(), row(), row(), row(), row(),
                  _const_spec((d, d)), _const_spec((d, d)), _const_spec((1, d)),
                  _const_spec((d, N_EXPERTS)), _const_spec((1, N_EXPERTS))],
        out_specs=[row(), row(), k4(), k4(), k4(), cnt()],
        out_shape=[jax.ShapeDtypeStruct((n, d), F32), jax.ShapeDtypeStruct((n, d), BF16),
                   jax.ShapeDtypeStruct((n, TOP_K), jnp.int32), jax.ShapeDtypeStruct((n, TOP_K), F32),
                   jax.ShapeDtypeStruct((n, TOP_K), jnp.int32),
                   jax.ShapeDtypeStruct((1, N_EXPERTS), F32)],
        scratch_shapes=[pltpu.VMEM((1, N_EXPERTS), F32)],
        compiler_params=_cparams(("arbitrary",), 32),
        name="post",
    )(cnt_in, x, on, sga, gcc, wap_b, wout_b, g, wr_b, br)


def _moe_kernel(te_ref, tv_ref, x_ref, wgu_ref, bg_ref, bu_ref, wd_ref, bd_ref, y_ref, wgu_sc, wd_sc):
    t = pl.program_id(0)
    valid = tv_ref[t] > 0
    new_expert = (t == 0) | (te_ref[t] != te_ref[jnp.maximum(t - 1, 0)])
    f = wd_ref.shape[1]
    pair = 2 * V_DIM

    @pl.when(valid & new_expert)
    def _():
        r = lax.broadcasted_iota(jnp.int32, (pair, pair), 0)
        c = lax.broadcasted_iota(jnp.int32, (pair, pair), 1)
        src = jnp.where(c < V_DIM, 2 * c, 2 * (c - V_DIM) + 1)
        perm = (r == src).astype(BF16)
        for b in range(2 * f // pair):
            blk = wgu_ref[0, :, b * pair:(b + 1) * pair].astype(BF16)
            wgu_sc[:, b * pair:(b + 1) * pair] = _dot(blk, perm).astype(BF16)
        wd_sc[...] = wd_ref[0].astype(BF16)

    @pl.when(valid)
    def _():
        hgu = _dot(x_ref[...], wgu_sc[...])
        acts = []
        for b in range(f // V_DIM):
            cols = slice(b * V_DIM, (b + 1) * V_DIM)
            gate = jnp.minimum(hgu[:, b * pair:b * pair + V_DIM] + bg_ref[0, :, cols], SWIGLU_LIMIT)
            up = jnp.clip(hgu[:, b * pair + V_DIM:(b + 1) * pair] + bu_ref[0, :, cols],
                          -SWIGLU_LIMIT, SWIGLU_LIMIT)
            glu = gate * _sigmoid(SWIGLU_ALPHA * gate)
            acts.append((glu * (up + 1.0)).astype(BF16))
        y = _dot(jnp.concatenate(acts, axis=1), wd_sc[...]) + bd_ref[0]
        y_ref[...] = y.astype(BF16)


def _moe(tile_expert, tile_valid, xs, w_gu, bg, bu, w_down, bd):
    p, d = xs.shape
    f = w_down.shape[1]
    n_tiles = p // MOE_TILE
    row = pl.BlockSpec((MOE_TILE, d), lambda i, te, tv: (i, 0))
    wspec = lambda a, b_: pl.BlockSpec((1, a, b_), lambda i, te, tv: (te[i], 0, 0))
    return pl.pallas_call(
        _moe_kernel,
        grid_spec=pltpu.PrefetchScalarGridSpec(
            num_scalar_prefetch=2,
            grid=(n_tiles,),
            in_specs=[row, wspec(d, 2 * f), wspec(1, f), wspec(1, f), wspec(f, d), wspec(1, d)],
            out_specs=row,
            scratch_shapes=[pltpu.VMEM((d, 2 * f), BF16), pltpu.VMEM((f, d), BF16)]),
        out_shape=jax.ShapeDtypeStruct((p, d), BF16),
        compiler_params=_cparams(("arbitrary",), 48),
        name="moe",
    )(tile_expert, tile_valid, xs, w_gu, bg, bu, w_down, bd)


def _final_kernel(h_ref, yg_ref, tw_ref, g_ref, o_ref):
    tw = tw_ref[...]
    acc = yg_ref[0].astype(F32) * tw[:, 0:1]
    for kk in range(1, TOP_K):
        acc = acc + yg_ref[kk].astype(F32) * tw[:, kk:kk + 1]
    h = h_ref[...] + acc
    ms = jnp.mean(h * h, axis=-1, keepdims=True)
    o_ref[...] = (h * lax.rsqrt(ms + EPS)) * g_ref[...]


def _final(h, yg, tw, g, row0):
    n, d = h.shape
    tm = min(n, ROW_TILE)
    assert n % tm == 0 and row0 % tm == 0
    blk0 = row0 // tm
    return pl.pallas_call(
        _final_kernel,
        grid=(n // tm,),
        in_specs=[pl.BlockSpec((tm, d), lambda i: (i, 0)),
                  pl.BlockSpec((TOP_K, tm, d), lambda i: (0, i + blk0, 0)),
                  pl.BlockSpec((tm, TOP_K), lambda i: (i, 0)),
                  _const_spec((1, d))],
        out_specs=pl.BlockSpec((tm, d), lambda i: (i, 0)),
        out_shape=jax.ShapeDtypeStruct((n, d), F32),
        compiler_params=_cparams(("arbitrary",), 32),
        name="final",
    )(h, yg, tw, g)


def kernel(x_prompt, x_sample, cache_k, cache_v, state_conv, g_attn_norm, w_in, lambda_q, lambda_k,
           g_subln, conv_w, w_attn_proj, w_conv_proj, w_out, g_ffn_norm, w_router, b_router, w_gu, b_gu,
           w_down, b_down, g_final):
    assert w_in.shape[0] == 1, "single-layer trunk"
    bp, tp, d = x_prompt.shape
    bs, ts, _ = x_sample.shape
    past = cache_k.shape[2]
    f = w_down.shape[2]
    n_p, n_s = bp * tp, bs * ts
    n_all = n_p + n_s

    w_in_b = w_in[0].astype(BF16)
    wcp_b = w_conv_proj[0].astype(BF16)
    wap_b = w_attn_proj[0].astype(BF16)
    wout_b = w_out[0].astype(BF16)
    wr_b = w_router[0].astype(BF16)
    bgu = b_gu[0].reshape(N_EXPERTS, 1, f, 2)
    bg, bu = bgu[..., 0], bgu[..., 1]
    bd = b_down[0].reshape(N_EXPERTS, 1, d)
    g_attn = g_attn_norm[0].reshape(1, d)
    g_ffn = g_ffn_norm[0].reshape(1, d)
    gs = g_subln[0].reshape(1, V_DIM)
    lq, lk = lambda_q[0], lambda_k[0]
    br = b_router[0].reshape(1, N_EXPERTS)

    prev_p = jnp.zeros((bp, 8, d), F32)
    prev_s = jnp.pad(state_conv[0], ((0, 0), (6, 0), (0, 0)))

    q_p, k_p, v_p, kb_p, vb_p, sga_p, gcc_p, conv_p = _inproj(x_prompt, prev_p, g_attn, w_in_b, conv_w[0], wcp_b)
    q_s, k_s, v_s, kb_s, vb_s, sga_s, gcc_s, conv_s = _inproj(x_sample, prev_s, g_attn, w_in_b, conv_w[0], wcp_b)

    on_p = _attn_prompt(lq, lk, gs, q_p, kb_p, vb_p)
    on_s = _attn_sample(lq, lk, gs, q_s, cache_k[0].reshape(bs, past, d), cache_v[0].reshape(bs, past, d),
                        kb_s, vb_s)

    flat = lambda a, n: a.reshape(n, d)
    cnt0 = jnp.zeros((1, N_EXPERTS), F32)
    h_p, hn_p, ti_p, tw_p, rk_p, cnt_p = _post(cnt0, flat(x_prompt, n_p), flat(on_p, n_p), flat(sga_p, n_p),
                                               flat(gcc_p, n_p), wap_b, wout_b, g_ffn, wr_b, br)
    h_s, hn_s, ti_s, tw_s, rk_s, cnt = _post(cnt_p, flat(x_sample, n_s), flat(on_s, n_s), flat(sga_s, n_s),
                                             flat(gcc_s, n_s), wap_b, wout_b, g_ffn, wr_b, br)

    counts = cnt[0].astype(jnp.int32)
    padded = ((counts + MOE_TILE - 1) // MOE_TILE) * MOE_TILE
    ends = jnp.cumsum(padded)
    starts = ends - padded
    n_tiles = (n_all * TOP_K) // MOE_TILE + N_EXPERTS
    tile_row = jnp.arange(n_tiles, dtype=jnp.int32) * MOE_TILE
    tile_expert = jnp.sum((tile_row[:, None] >= ends[None, :]).astype(jnp.int32), axis=1)
    tile_expert = jnp.minimum(tile_expert, N_EXPERTS - 1)
    tile_valid = (tile_row < ends[-1]).astype(jnp.int32)

    top_i = jnp.concatenate([ti_p, ti_s], axis=0)
    rank = jnp.concatenate([rk_p, rk_s], axis=0)
    pos = starts[top_i] + rank
    hn_all = jnp.concatenate([hn_p, hn_s], axis=0)

    token = jnp.broadcast_to(jnp.arange(n_all, dtype=jnp.int32)[:, None], (n_all, TOP_K))
    src = jnp.zeros((n_tiles * MOE_TILE,), jnp.int32).at[pos.reshape(-1)].set(token.reshape(-1))
    xs = hn_all.at[src].get(mode="promise_in_bounds")

    ys = _moe(tile_expert, tile_valid, xs, w_gu[0], bg, bu, w_down[0], bd)
    yg = ys.at[pos.T].get(mode="promise_in_bounds")

    y_p = _final(h_p, yg, tw_p, g_final.reshape(1, d), 0)
    y_s = _final(h_s, yg, tw_s, g_final.reshape(1, d), n_p)

    hd = N_HEADS
    return (y_p.reshape(bp, tp, d), y_s.reshape(bs, ts, d),
            k_p.reshape(1, bp, tp, hd, V_DIM), v_p.reshape(1, bp, tp, hd, V_DIM), conv_p[None],
            k_s.reshape(1, bs, ts, hd, V_DIM), v_s.reshape(1, bs, ts, hd, V_DIM), conv_s[None])
```

```python
import functools
import math

import jax
import jax.numpy as jnp
from jax import lax
from jax.experimental import pallas as pl
from jax.experimental.pallas import tpu as pltpu
from jax.experimental.pallas import tpu_sc as plsc

N_HEADS = 8
HEAD_DIM = 64
V_DIM = 2 * HEAD_DIM
CHUNK = 64
N_EXPERTS = 32
TOP_K = 4
SWIGLU_LIMIT = 7.0
SWIGLU_ALPHA = 1.702
EPS = 1e-6
NEG = -1e30
LAMBDA_INIT = 0.8 - 0.6 * math.exp(-0.3 * 0)
N_SEG = 8
Q_SCALE = HEAD_DIM ** -0.5 * math.log2(math.e)

ROW_TILE = 256
ATTN_TILE = 512
MOE_TILE = 256
MIB = 1024 * 1024
SC_CORES = 2
SC_SUBCORES = 16
SC_WORKERS = SC_CORES * SC_SUBCORES
SC_MAX_CHUNK = 32

BF16 = jnp.bfloat16
F32 = jnp.float32


def _cparams(semantics, vmem_mib):
    return pltpu.CompilerParams(dimension_semantics=semantics, vmem_limit_bytes=vmem_mib * MIB)


def _const_spec(shape):
    nd = len(shape)
    return pl.BlockSpec(shape, lambda *_: (0,) * nd, pipeline_mode=pl.Buffered(1))


def _sigmoid(x):
    return 1.0 / (1.0 + jnp.exp(-x))


def _dot(a, b):
    return jnp.dot(a, b, preferred_element_type=F32)


def _dot_nt(a, b):
    return lax.dot_general(a, b, (((1,), (1,)), ((), ())), preferred_element_type=F32)


def _pack_halves(x):
    w = x.shape[1] // 2
    lo = lax.bitcast_convert_type(x[:, :w].astype(BF16).astype(F32), jnp.uint32)
    hi = lax.bitcast_convert_type(x[:, w:].astype(BF16).astype(F32), jnp.uint32)
    return lax.bitcast_convert_type((lo >> 16) | (hi & jnp.uint32(0xFFFF0000)), jnp.int32)


def _unpack_halves(words):
    u = lax.bitcast_convert_type(words, jnp.uint32)
    lo = lax.bitcast_convert_type(u << 16, F32)
    hi = lax.bitcast_convert_type(u & jnp.uint32(0xFFFF0000), F32)
    return lo, hi


def _inproj_kernel(x_ref, prev_ref, g_ref, w_ref, cw_ref, wcp_ref,
                   q_ref, k32_ref, v32_ref, kb_ref, vb_ref, sga_ref, gcc_ref, cnew_ref, ubuf):
    t = pl.program_id(1)
    tm, d = x_ref.shape[1], x_ref.shape[2]

    @pl.when(t == 0)
    def _():
        ubuf[0:8, :] = prev_ref[0]

    x = x_ref[0]
    ms = jnp.mean(x * x, axis=-1, keepdims=True)
    xn = ((x * lax.rsqrt(ms + EPS)) * g_ref[...]).astype(BF16)

    def seg(i):
        return _dot(xn, w_ref[:, i * d:(i + 1) * d])

    q_ref[0] = (seg(0) * Q_SCALE).astype(BF16)
    k = seg(1)
    k32_ref[0] = k
    kb_ref[0] = k.astype(BF16)
    v = seg(2)
    v32_ref[0] = v
    vb_ref[0] = v.astype(BF16)

    u = seg(5) * seg(3)
    ubuf[8:8 + tm, :] = u
    conv = (ubuf[6:6 + tm, :] * cw_ref[0:1, :] + ubuf[7:7 + tm, :] * cw_ref[1:2, :]
            + u * cw_ref[2:3, :])
    cpre = (seg(4) * conv).astype(BF16)
    c_out = _dot(cpre, wcp_ref[...])
    gcc_ref[0] = (_sigmoid(seg(7)) * c_out).astype(BF16)
    sga_ref[0] = _sigmoid(seg(6)).astype(BF16)
    cnew_ref[0] = ubuf[tm + 6:tm + 8, :]
    ubuf[0:8, :] = ubuf[tm:tm + 8, :]


def _inproj(x, conv_prev8, g, w_in_b, conv_w, wcp_b):
    b, t, d = x.shape
    tm = min(t, ROW_TILE)
    assert t % tm == 0 and tm % 8 == 0
    row = lambda: pl.BlockSpec((1, tm, d), lambda i, j: (i, j, 0))
    bf = jax.ShapeDtypeStruct((b, t, d), BF16)
    f32 = jax.ShapeDtypeStruct((b, t, d), F32)
    return pl.pallas_call(
        _inproj_kernel,
        grid=(b, t // tm),
        in_specs=[row(),
                  pl.BlockSpec((1, 8, d), lambda i, j: (i, 0, 0)),
                  _const_spec((1, d)),
                  _const_spec((d, N_SEG * d)),
                  _const_spec((3, d)),
                  _const_spec((d, d))],
        out_specs=[row(), row(), row(), row(), row(), row(), row(),
                   pl.BlockSpec((1, 2, d), lambda i, j: (i, 0, 0))],
        out_shape=[bf, f32, f32, bf, bf, bf, bf, jax.ShapeDtypeStruct((b, 2, d), F32)],
        scratch_shapes=[pltpu.VMEM((tm + 8, d), F32)],
        compiler_params=_cparams(("arbitrary", "arbitrary"), 56),
        name="inproj",
    )(x, conv_prev8, g, w_in_b, conv_w, wcp_b)


def _lambda(lq_ref, lk_ref):
    e = jnp.exp(jnp.sum(lq_ref[...] * lk_ref[...], axis=-1, keepdims=True))
    return e[0:1, :] - e[1:2, :] + LAMBDA_INIT


def _stack_maps(q):
    lane = lax.broadcasted_iota(jnp.int32, q.shape, 1)
    zero = jnp.zeros_like(q)
    return jnp.concatenate([jnp.where(lane < HEAD_DIM, q, zero),
                            jnp.where(lane >= HEAD_DIM, q, zero)], axis=0)


def _attn_finish(acc, l, lam, gs, tq):
    o = acc / l
    o = o[:tq] - lam * o[tq:]
    ms = jnp.mean(o * o, axis=-1, keepdims=True)
    return ((o * lax.rsqrt(ms + EPS)) * gs) * (1.0 - LAMBDA_INIT)


def _attn_prompt_kernel(lq_ref, lk_ref, gs_ref, q_ref, k_ref, v_ref, o_ref,
                        s_sc, p_sc, m_sc, l_sc, acc_sc, *, tile):
    i = pl.program_id(2)
    qq = _stack_maps(q_ref[0])
    lane_tiles = [slice(c * V_DIM, (c + 1) * V_DIM) for c in range(tile // V_DIM)]

    def kv_tile(ref, j):
        return ref[0, pl.ds(pl.multiple_of(j * tile, tile), tile), :]

    def scores(j, masked):
        s = _dot_nt(qq, kv_tile(k_ref, j))
        if masked:
            r = lax.broadcasted_iota(jnp.int32, s.shape, 0)
            c = lax.broadcasted_iota(jnp.int32, s.shape, 1)
            r = jnp.where(r >= tile, r - tile, r)
            s = jnp.where((c // CHUNK) <= (r // CHUNK), s, NEG)
        s_sc[...] = s

    def prev_pv(j):
        return _dot(p_sc[...], kv_tile(v_ref, j))

    def softmax(pv):
        m_cur = s_sc[:, lane_tiles[0]]
        for lt in lane_tiles[1:]:
            m_cur = jnp.maximum(m_cur, s_sc[:, lt])
        m_old = m_sc[...]
        m_new = jnp.maximum(m_old, jnp.max(m_cur, axis=-1, keepdims=True))
        alpha = jnp.exp2(m_old - m_new)
        psum = None
        for lt in lane_tiles:
            pc = jnp.exp2(s_sc[:, lt] - m_new)
            p_sc[:, lt] = pc.astype(BF16)
            psum = pc if psum is None else psum + pc
        l_sc[...] = alpha * l_sc[...] + psum
        acc_sc[...] = alpha * (acc_sc[...] + pv)
        m_sc[...] = m_new

    def tile_of(k):
        return jnp.where(k == 0, i, k - 1)

    m_sc[...] = jnp.full(m_sc.shape, NEG, F32)
    l_sc[...] = jnp.zeros(l_sc.shape, F32)
    acc_sc[...] = jnp.zeros(acc_sc.shape, F32)
    p_sc[...] = jnp.zeros(p_sc.shape, BF16)
    scores(i, True)

    def body(k, carry):
        pv = prev_pv(tile_of(jnp.maximum(k - 1, 0)))
        softmax(pv)
        scores(jnp.minimum(k, jnp.maximum(i - 1, 0)), False)
        return carry

    lax.fori_loop(0, i + 1, body, 0)
    acc = acc_sc[...] + prev_pv(tile_of(i))
    l = jnp.sum(l_sc[...], axis=-1, keepdims=True)
    o_ref[0] = _attn_finish(acc, l, _lambda(lq_ref, lk_ref), gs_ref[...], tile).astype(BF16)


def _attn_prompt(lq, lk, gs, q, kb, vb):
    b, t, d = q.shape
    tile = min(t, ATTN_TILE)
    assert t % tile == 0 and tile % CHUNK == 0 and tile % V_DIM == 0
    qspec = pl.BlockSpec((1, tile, V_DIM), lambda bi, h, i: (bi, i, h))
    kspec = pl.BlockSpec((1, t, V_DIM), lambda bi, h, i: (bi, 0, h))
    rows = 2 * tile
    return pl.pallas_call(
        functools.partial(_attn_prompt_kernel, tile=tile),
        grid=(b, N_HEADS, t // tile),
        in_specs=[_const_spec((2, HEAD_DIM)), _const_spec((2, HEAD_DIM)), _const_spec((1, V_DIM)),
                  qspec, kspec, kspec],
        out_specs=qspec,
        out_shape=jax.ShapeDtypeStruct((b, t, d), BF16),
        scratch_shapes=[pltpu.VMEM((rows, tile), F32), pltpu.VMEM((rows, tile), BF16),
                        pltpu.VMEM((rows, V_DIM), F32), pltpu.VMEM((rows, V_DIM), F32),
                        pltpu.VMEM((rows, V_DIM), F32)],
        compiler_params=_cparams(("arbitrary", "arbitrary", "arbitrary"), 40),
        name="attn_prompt",
    )(lq, lk, gs, q, kb, vb)


def _attn_sample_kernel(lq_ref, lk_ref, gs_ref, q_ref, ck_ref, cv_ref, kn_ref, vn_ref, o_ref):
    tq = q_ref.shape[1]
    qq = _stack_maps(q_ref[0])
    s_p = _dot_nt(qq, ck_ref[0].astype(BF16))
    s_n = _dot_nt(qq, kn_ref[0])
    m = jnp.maximum(jnp.max(s_p, axis=-1, keepdims=True), jnp.max(s_n, axis=-1, keepdims=True))
    p_p = jnp.exp2(s_p - m)
    p_n = jnp.exp2(s_n - m)
    l = jnp.sum(p_p, axis=-1, keepdims=True) + jnp.sum(p_n, axis=-1, keepdims=True)
    acc = _dot(p_p.astype(BF16), cv_ref[0].astype(BF16)) + _dot(p_n.astype(BF16), vn_ref[0])
    o_ref[0] = _attn_finish(acc, l, _lambda(lq_ref, lk_ref), gs_ref[...], tq).astype(BF16)


def _attn_sample(lq, lk, gs, q, cache_k, cache_v, kb, vb):
    b, t, d = q.shape
    past = cache_k.shape[1]
    qspec = pl.BlockSpec((1, t, V_DIM), lambda bi, h: (bi, 0, h))
    cspec = pl.BlockSpec((1, past, V_DIM), lambda bi, h: (bi, 0, h))
    return pl.pallas_call(
        _attn_sample_kernel,
        grid=(b, N_HEADS),
        in_specs=[_const_spec((2, HEAD_DIM)), _const_spec((2, HEAD_DIM)), _const_spec((1, V_DIM)),
                  qspec, cspec, cspec, qspec, qspec],
        out_specs=qspec,
        out_shape=jax.ShapeDtypeStruct((b, t, d), BF16),
        compiler_params=_cparams(("arbitrary", "arbitrary"), 32),
        name="attn_sample",
    )(lq, lk, gs, q, cache_k, cache_v, kb, vb)


def _cols4(cols, dtype):
    tm = cols[0].shape[0]
    lane = lax.broadcasted_iota(jnp.int32, (tm, TOP_K), 1)
    out = jnp.broadcast_to(cols[3], (tm, TOP_K))
    for kk in (2, 1, 0):
        out = jnp.where(lane == kk, cols[kk], out)
    return out.astype(dtype)


def _post_kernel(cnt_in_ref, x_ref, on_ref, sga_ref, gcc_ref, wap_ref, wout_ref, g_ref, wr_ref, br_ref,
                 h_ref, hn_ref, ti_ref, tw_ref, rk_ref, cnt_ref, carry):
    step = pl.program_id(0)
    tm = x_ref.shape[0]

    @pl.when(step == 0)
    def _():
        carry[...] = cnt_in_ref[...]

    a_out = _dot(on_ref[...], wap_ref[...])
    merged = sga_ref[...].astype(F32) * a_out + gcc_ref[...].astype(F32)
    h = x_ref[...] + _dot(merged.astype(BF16), wout_ref[...])
    h_ref[...] = h
    ms = jnp.mean(h * h, axis=-1, keepdims=True)
    hn32 = (h * lax.rsqrt(ms + EPS)) * g_ref[...]
    hn_ref[...] = _pack_halves(hn32)
    hn = hn32.astype(BF16)

    logits = _dot(hn, wr_ref[...]) + br_ref[...]
    lane = lax.broadcasted_iota(jnp.int32, logits.shape, 1)
    work = logits
    vals, idxs, sels = [], [], []
    for _ in range(TOP_K):
        mk = jnp.max(work, axis=-1, keepdims=True)
        ik = jnp.min(jnp.where(work == mk, lane, N_EXPERTS), axis=-1, keepdims=True)
        sel = lane == ik
        work = jnp.where(sel, -jnp.inf, work)
        vals.append(mk)
        idxs.append(ik)
        sels.append(sel)
    es = [jnp.exp(vk - vals[0]) for vk in vals]
    denom = es[0] + es[1] + es[2] + es[3]
    ti_ref[...] = _cols4(idxs, jnp.int32)
    tw_ref[...] = _cols4([e / denom for e in es], F32)

    picked = (sels[0] | sels[1] | sels[2] | sels[3])
    r = lax.broadcasted_iota(jnp.int32, (tm, tm), 0)
    c = lax.broadcasted_iota(jnp.int32, (tm, tm), 1)
    before = _dot((c < r).astype(BF16), picked.astype(BF16)) + carry[...]
    ranks = [jnp.sum(jnp.where(sk, before, 0.0), axis=-1, keepdims=True) for sk in sels]
    rk_ref[...] = _cols4(ranks, jnp.int32)
    carry[...] = carry[...] + jnp.sum(picked.astype(F32), axis=0, keepdims=True)
    cnt_ref[...] = carry[...]


def _post(cnt_in, x, on, sga, gcc, wap_b, wout_b, g, wr_b, br):
    n, d = x.shape
    tm = min(n, ROW_TILE)
    assert n % tm == 0
    row = lambda: pl.BlockSpec((tm, d), lambda i: (i, 0))
    k4 = lambda: pl.BlockSpec((tm, TOP_K), lambda i: (i, 0))
    cnt = lambda: pl.BlockSpec((1, N_EXPERTS), lambda i: (0, 0))
    return pl.pallas_call(
        _post_kernel,
        grid=(n // tm,),
        in_specs=[cnt(), row(), row(), row(), row(),
                  _const_spec((d, d)), _const_spec((d, d)), _const_spec((1, d)),
                  _const_spec((d, N_EXPERTS)), _const_spec((1, N_EXPERTS))],
        out_specs=[row(), pl.BlockSpec((tm, d // 2), lambda i: (i, 0)), k4(), k4(), k4(), cnt()],
        out_shape=[jax.ShapeDtypeStruct((n, d), F32), jax.ShapeDtypeStruct((n, d // 2), jnp.int32),
                   jax.ShapeDtypeStruct((n, TOP_K), jnp.int32), jax.ShapeDtypeStruct((n, TOP_K), F32),
                   jax.ShapeDtypeStruct((n, TOP_K), jnp.int32),
                   jax.ShapeDtypeStruct((1, N_EXPERTS), F32)],
        scratch_shapes=[pltpu.VMEM((1, N_EXPERTS), F32)],
        compiler_params=_cparams(("arbitrary",), 32),
        name="post",
    )(cnt_in, x, on, sga, gcc, wap_b, wout_b, g, wr_b, br)


def _moe_kernel(te_ref, tv_ref, x_ref, wgu_ref, bg_ref, bu_ref, wd_ref, bd_ref, y_ref, wgu_sc, wd_sc):
    t = pl.program_id(0)
    valid = tv_ref[t] > 0
    new_expert = (t == 0) | (te_ref[t] != te_ref[jnp.maximum(t - 1, 0)])
    f = wd_ref.shape[1]
    pair = 2 * V_DIM

    @pl.when(valid & new_expert)
    def _():
        r = lax.broadcasted_iota(jnp.int32, (pair, pair), 0)
        c = lax.broadcasted_iota(jnp.int32, (pair, pair), 1)
        src = jnp.where(c < V_DIM, 2 * c, 2 * (c - V_DIM) + 1)
        perm = (r == src).astype(BF16)
        for b in range(2 * f // pair):
            blk = wgu_ref[0, :, b * pair:(b + 1) * pair].astype(BF16)
            wgu_sc[:, b * pair:(b + 1) * pair] = _dot(blk, perm).astype(BF16)
        wd_sc[...] = wd_ref[0].astype(BF16)

    @pl.when(valid)
    def _():
        x_lo, x_hi = _unpack_halves(x_ref[...])
        x = jnp.concatenate([x_lo.astype(BF16), x_hi.astype(BF16)], axis=1)
        hgu = _dot(x, wgu_sc[...])
        acts = []
        for b in range(f // V_DIM):
            cols = slice(b * V_DIM, (b + 1) * V_DIM)
            gate = jnp.minimum(hgu[:, b * pair:b * pair + V_DIM] + bg_ref[0, :, cols], SWIGLU_LIMIT)
            up = jnp.clip(hgu[:, b * pair + V_DIM:(b + 1) * pair] + bu_ref[0, :, cols],
                          -SWIGLU_LIMIT, SWIGLU_LIMIT)
            glu = gate * _sigmoid(SWIGLU_ALPHA * gate)
            acts.append((glu * (up + 1.0)).astype(BF16))
        y = _dot(jnp.concatenate(acts, axis=1), wd_sc[...]) + bd_ref[0]
        y_ref[...] = _pack_halves(y)


def _moe(tile_expert, tile_valid, xs, w_gu, bg, bu, w_down, bd):
    p = xs.shape[0]
    f, d = w_down.shape[1:]
    n_tiles = p // MOE_TILE
    row = pl.BlockSpec((MOE_TILE, d // 2), lambda i, te, tv: (i, 0))
    wspec = lambda a, b_: pl.BlockSpec((1, a, b_), lambda i, te, tv: (te[i], 0, 0))
    return pl.pallas_call(
        _moe_kernel,
        grid_spec=pltpu.PrefetchScalarGridSpec(
            num_scalar_prefetch=2,
            grid=(n_tiles,),
            in_specs=[row, wspec(d, 2 * f), wspec(1, f), wspec(1, f), wspec(f, d), wspec(1, d)],
            out_specs=row,
            scratch_shapes=[pltpu.VMEM((d, 2 * f), BF16), pltpu.VMEM((f, d), BF16)]),
        out_shape=jax.ShapeDtypeStruct((p, d // 2), jnp.int32),
        compiler_params=_cparams(("arbitrary",), 48),
        name="moe",
    )(tile_expert, tile_valid, xs, w_gu, bg, bu, w_down, bd)


def _sc_split(n):
    per_w = n // SC_WORKERS
    assert per_w * SC_WORKERS == n and per_w % 8 == 0, n
    ch = max(c for c in range(8, SC_MAX_CHUNK + 1, 8) if per_w % c == 0)
    return per_w, ch


def _sc_positions(pos):
    per_w, ch = _sc_split(pos.shape[0])
    return pos.reshape(SC_WORKERS, per_w // ch, ch, TOP_K).transpose(0, 1, 3, 2)


def _sc_mesh():
    return plsc.VectorSubcoreMesh(core_axis_name="c", subcore_axis_name="s",
                                  num_cores=SC_CORES, num_subcores=SC_SUBCORES)


def _sc_worker():
    return lax.axis_index("s") * SC_CORES + lax.axis_index("c")


def _sc_dispatch(srcs, poss, p_rows):
    w = srcs[0].shape[1]
    geo = [_sc_split(s.shape[0]) for s in srcs]
    nseg = len(srcs)
    scratch = []
    for _, ch in geo:
        scratch += [pltpu.VMEM((TOP_K, ch), jnp.int32), pltpu.VMEM((ch, w), jnp.int32)]
    scratch.append(pltpu.SemaphoreType.DMA)

    def body(*refs):
        src_refs, pos_refs, xs_ref = refs[:nseg], refs[nseg:2 * nseg], refs[2 * nseg]
        scr = refs[2 * nseg + 1:]
        sem = scr[-1]
        wid = _sc_worker()
        for g in range(nseg):
            per_w, ch = geo[g]

            @pl.loop(0, per_w // ch)
            def _(c, src=src_refs[g], pos=pos_refs[g], idx_v=scr[2 * g], rows_v=scr[2 * g + 1],
                  per_w=per_w, ch=ch):
                base = pl.multiple_of(wid * per_w + c * ch, 8)
                pltpu.sync_copy(pos.at[wid, c], idx_v)
                pltpu.sync_copy(src.at[pl.ds(base, ch)], rows_v)
                copies = [pltpu.async_copy(rows_v, xs_ref.at[idx_v.at[kk]], sem) for kk in range(TOP_K)]
                for cp in copies:
                    cp.wait()

    return pl.kernel(body, out_type=jax.ShapeDtypeStruct((p_rows, w), jnp.int32), mesh=_sc_mesh(),
                     scratch_types=scratch, name="sc_dispatch")(*srcs, *poss)


def _sc_gather(ys, poss, seg_rows):
    w = ys.shape[1]
    geo = [_sc_split(n) for n in seg_rows]
    nseg = len(seg_rows)
    row0 = [sum(seg_rows[:g]) for g in range(nseg)]
    scratch = []
    for _, ch in geo:
        scratch += [pltpu.VMEM((TOP_K, ch), jnp.int32), pltpu.VMEM((TOP_K, ch, w), jnp.int32)]
    scratch += [pltpu.SemaphoreType.DMA, pltpu.SemaphoreType.DMA]

    def body(*refs):
        ys_ref, pos_refs, out_ref = refs[0], refs[1:1 + nseg], refs[1 + nseg]
        scr = refs[2 + nseg:]
        sem_g, sem_w = scr[-2], scr[-1]
        wid = _sc_worker()
        for g in range(nseg):
            per_w, ch = geo[g]

            @pl.loop(0, per_w // ch)
            def _(c, pos=pos_refs[g], idx_v=scr[2 * g], rows_v=scr[2 * g + 1], per_w=per_w, ch=ch,
                  r0=row0[g]):
                base = pl.multiple_of(r0 + wid * per_w + c * ch, 8)
                pltpu.sync_copy(pos.at[wid, c], idx_v)
                gathers = [pltpu.async_copy(ys_ref.at[idx_v.at[kk]], rows_v.at[kk], sem_g)
                           for kk in range(TOP_K)]
                writes = []
                for kk in range(TOP_K):
                    gathers[kk].wait()
                    writes.append(pltpu.async_copy(rows_v.at[kk], out_ref.at[kk, pl.ds(base, ch)], sem_w))
                for cp in writes:
                    cp.wait()

    return pl.kernel(body, out_type=jax.ShapeDtypeStruct((TOP_K, sum(seg_rows), w), jnp.int32),
                     mesh=_sc_mesh(), scratch_types=scratch, name="sc_gather")(ys, *poss)


def _final_kernel(h_ref, yg_ref, tw_ref, g_ref, o_ref):
    tw = tw_ref[...]
    acc = None
    for kk in range(TOP_K):
        lo, hi = _unpack_halves(yg_ref[kk])
        part = jnp.concatenate([lo, hi], axis=1) * tw[:, kk:kk + 1]
        acc = part if acc is None else acc + part
    h = h_ref[...] + acc
    ms = jnp.mean(h * h, axis=-1, keepdims=True)
    o_ref[...] = (h * lax.rsqrt(ms + EPS)) * g_ref[...]


def _final(h, yg, tw, g, row0):
    n, d = h.shape
    tm = min(n, ROW_TILE)
    assert n % tm == 0 and row0 % tm == 0
    blk0 = row0 // tm
    return pl.pallas_call(
        _final_kernel,
        grid=(n // tm,),
        in_specs=[pl.BlockSpec((tm, d), lambda i: (i, 0)),
                  pl.BlockSpec((TOP_K, tm, d // 2), lambda i: (0, i + blk0, 0)),
                  pl.BlockSpec((tm, TOP_K), lambda i: (i, 0)),
                  _const_spec((1, d))],
        out_specs=pl.BlockSpec((tm, d), lambda i: (i, 0)),
        out_shape=jax.ShapeDtypeStruct((n, d), F32),
        compiler_params=_cparams(("arbitrary",), 32),
        name="final",
    )(h, yg, tw, g)


def kernel(x_prompt, x_sample, cache_k, cache_v, state_conv, g_attn_norm, w_in, lambda_q, lambda_k,
           g_subln, conv_w, w_attn_proj, w_conv_proj, w_out, g_ffn_norm, w_router, b_router, w_gu, b_gu,
           w_down, b_down, g_final):
    assert w_in.shape[0] == 1, "single-layer trunk"
    bp, tp, d = x_prompt.shape
    bs, ts, _ = x_sample.shape
    past = cache_k.shape[2]
    f = w_down.shape[2]
    n_p, n_s = bp * tp, bs * ts
    n_all = n_p + n_s

    w_in_b = w_in[0].astype(BF16)
    wcp_b = w_conv_proj[0].astype(BF16)
    wap_b = w_attn_proj[0].astype(BF16)
    wout_b = w_out[0].astype(BF16)
    wr_b = w_router[0].astype(BF16)
    bgu = b_gu[0].reshape(N_EXPERTS, 1, f, 2)
    bg, bu = bgu[..., 0], bgu[..., 1]
    bd = b_down[0].reshape(N_EXPERTS, 1, d)
    g_attn = g_attn_norm[0].reshape(1, d)
    g_ffn = g_ffn_norm[0].reshape(1, d)
    gs = g_subln[0].reshape(1, V_DIM)
    lq, lk = lambda_q[0], lambda_k[0]
    br = b_router[0].reshape(1, N_EXPERTS)

    prev_p = jnp.zeros((bp, 8, d), F32)
    prev_s = jnp.pad(state_conv[0], ((0, 0), (6, 0), (0, 0)))

    q_p, k_p, v_p, kb_p, vb_p, sga_p, gcc_p, conv_p = _inproj(x_prompt, prev_p, g_attn, w_in_b, conv_w[0], wcp_b)
    q_s, k_s, v_s, kb_s, vb_s, sga_s, gcc_s, conv_s = _inproj(x_sample, prev_s, g_attn, w_in_b, conv_w[0], wcp_b)

    on_p = _attn_prompt(lq, lk, gs, q_p, kb_p, vb_p)
    on_s = _attn_sample(lq, lk, gs, q_s, cache_k[0].reshape(bs, past, d), cache_v[0].reshape(bs, past, d),
                        kb_s, vb_s)

    flat = lambda a, n: a.reshape(n, d)
    cnt0 = jnp.zeros((1, N_EXPERTS), F32)
    h_p, hn_p, ti_p, tw_p, rk_p, cnt_p = _post(cnt0, flat(x_prompt, n_p), flat(on_p, n_p), flat(sga_p, n_p),
                                               flat(gcc_p, n_p), wap_b, wout_b, g_ffn, wr_b, br)
    h_s, hn_s, ti_s, tw_s, rk_s, cnt = _post(cnt_p, flat(x_sample, n_s), flat(on_s, n_s), flat(sga_s, n_s),
                                             flat(gcc_s, n_s), wap_b, wout_b, g_ffn, wr_b, br)

    counts = cnt[0].astype(jnp.int32)
    padded = ((counts + MOE_TILE - 1) // MOE_TILE) * MOE_TILE
    ends = jnp.cumsum(padded)
    starts = ends - padded
    n_tiles = (n_all * TOP_K) // MOE_TILE + N_EXPERTS
    tile_row = jnp.arange(n_tiles, dtype=jnp.int32) * MOE_TILE
    tile_expert = jnp.sum((tile_row[:, None] >= ends[None, :]).astype(jnp.int32), axis=1)
    tile_expert = jnp.minimum(tile_expert, N_EXPERTS - 1)
    tile_valid = (tile_row < ends[-1]).astype(jnp.int32)

    poss = [_sc_positions(starts[ti_p] + rk_p), _sc_positions(starts[ti_s] + rk_s)]
    xs = _sc_dispatch([hn_p, hn_s], poss, n_tiles * MOE_TILE)
    ys = _moe(tile_expert, tile_valid, xs, w_gu[0], bg, bu, w_down[0], bd)
    yg = _sc_gather(ys, poss, [n_p, n_s])

    y_p = _final(h_p, yg, tw_p, g_final.reshape(1, d), 0)
    y_s = _final(h_s, yg, tw_s, g_final.reshape(1, d), n_p)

    hd = N_HEADS
    return (y_p.reshape(bp, tp, d), y_s.reshape(bs, ts, d),
            k_p.reshape(1, bp, tp, hd, V_DIM), v_p.reshape(1, bp, tp, hd, V_DIM), conv_p[None],
            k_s.reshape(1, bs, ts, hd, V_DIM), v_s.reshape(1, bs, ts, hd, V_DIM), conv_s[None])
```

```python
import functools
import math

import jax
import jax.numpy as jnp
from jax import lax
from jax.experimental import pallas as pl
from jax.experimental.pallas import tpu as pltpu
from jax.experimental.pallas import tpu_sc as plsc

N_HEADS = 8
HEAD_DIM = 64
V_DIM = 2 * HEAD_DIM
CHUNK = 64
N_EXPERTS = 32
TOP_K = 4
SWIGLU_LIMIT = 7.0
SWIGLU_ALPHA = 1.702
EPS = 1e-6
NEG = -1e30
LAMBDA_INIT = 0.8 - 0.6 * math.exp(-0.3 * 0)
N_SEG = 8
Q_SCALE = HEAD_DIM ** -0.5 * math.log2(math.e)

ROW_TILE = 256
STREAM_TILE = 512
ATTN_TILE = 512
MOE_TILE = 512
MIB = 1024 * 1024
SC_CORES = 2
SC_SUBCORES = 16
SC_WORKERS = SC_CORES * SC_SUBCORES
SC_MAX_CHUNK = 32

BF16 = jnp.bfloat16
F32 = jnp.float32


def _cparams(semantics, vmem_mib):
    return pltpu.CompilerParams(dimension_semantics=semantics, vmem_limit_bytes=vmem_mib * MIB)


def _const_spec(shape):
    nd = len(shape)
    return pl.BlockSpec(shape, lambda *_: (0,) * nd, pipeline_mode=pl.Buffered(1))


def _sigmoid(x):
    return 1.0 / (1.0 + jnp.exp(-x))


def _dot(a, b):
    return jnp.dot(a, b, preferred_element_type=F32)


def _dot_nt(a, b):
    return lax.dot_general(a, b, (((1,), (1,)), ((), ())), preferred_element_type=F32)


def _pack_halves(x):
    w = x.shape[1] // 2
    lo = lax.bitcast_convert_type(x[:, :w].astype(BF16).astype(F32), jnp.uint32)
    hi = lax.bitcast_convert_type(x[:, w:].astype(BF16).astype(F32), jnp.uint32)
    return lax.bitcast_convert_type((lo >> 16) | (hi & jnp.uint32(0xFFFF0000)), jnp.int32)


def _unpack_halves(words):
    u = lax.bitcast_convert_type(words, jnp.uint32)
    lo = lax.bitcast_convert_type(u << 16, F32)
    hi = lax.bitcast_convert_type(u & jnp.uint32(0xFFFF0000), F32)
    return lo, hi


def _inproj_kernel(x_ref, prev_ref, g_ref, w_ref, cw_ref, wcp_ref,
                   q_ref, k32_ref, v32_ref, kb_ref, vb_ref, sga_ref, gcc_ref, cnew_ref, ubuf):
    t = pl.program_id(1)
    tm, d = x_ref.shape[1], x_ref.shape[2]

    @pl.when(t == 0)
    def _():
        ubuf[0:8, :] = prev_ref[0]

    x = x_ref[0]
    ms = jnp.mean(x * x, axis=-1, keepdims=True)
    xn = ((x * lax.rsqrt(ms + EPS)) * g_ref[...]).astype(BF16)

    def seg(i):
        return _dot(xn, w_ref[:, i * d:(i + 1) * d])

    q_ref[0] = (seg(0) * Q_SCALE).astype(BF16)
    k = seg(1)
    k32_ref[0] = k
    kb_ref[0] = k.astype(BF16)
    v = seg(2)
    v32_ref[0] = v
    vb_ref[0] = v.astype(BF16)

    u = seg(5) * seg(3)
    ubuf[8:8 + tm, :] = u
    conv = (ubuf[6:6 + tm, :] * cw_ref[0:1, :] + ubuf[7:7 + tm, :] * cw_ref[1:2, :]
            + u * cw_ref[2:3, :])
    cpre = (seg(4) * conv).astype(BF16)
    c_out = _dot(cpre, wcp_ref[...])
    gcc_ref[0] = (_sigmoid(seg(7)) * c_out).astype(BF16)
    sga_ref[0] = _sigmoid(seg(6)).astype(BF16)
    cnew_ref[0] = ubuf[tm + 6:tm + 8, :]
    ubuf[0:8, :] = ubuf[tm:tm + 8, :]


def _inproj(x, conv_prev8, g, w_in_b, conv_w, wcp_b):
    b, t, d = x.shape
    tm = min(t, ROW_TILE)
    assert t % tm == 0 and tm % 8 == 0
    row = lambda: pl.BlockSpec((1, tm, d), lambda i, j: (i, j, 0))
    bf = jax.ShapeDtypeStruct((b, t, d), BF16)
    f32 = jax.ShapeDtypeStruct((b, t, d), F32)
    return pl.pallas_call(
        _inproj_kernel,
        grid=(b, t // tm),
        in_specs=[row(),
                  pl.BlockSpec((1, 8, d), lambda i, j: (i, 0, 0)),
                  _const_spec((1, d)),
                  _const_spec((d, N_SEG * d)),
                  _const_spec((3, d)),
                  _const_spec((d, d))],
        out_specs=[row(), row(), row(), row(), row(), row(), row(),
                   pl.BlockSpec((1, 2, d), lambda i, j: (i, 0, 0))],
        out_shape=[bf, f32, f32, bf, bf, bf, bf, jax.ShapeDtypeStruct((b, 2, d), F32)],
        scratch_shapes=[pltpu.VMEM((tm + 8, d), F32)],
        compiler_params=_cparams(("arbitrary", "arbitrary"), 56),
        name="inproj",
    )(x, conv_prev8, g, w_in_b, conv_w, wcp_b)


def _lambda(lq_ref, lk_ref):
    e = jnp.exp(jnp.sum(lq_ref[...] * lk_ref[...], axis=-1, keepdims=True))
    return e[0:1, :] - e[1:2, :] + LAMBDA_INIT


def _stack_maps(q):
    lane = lax.broadcasted_iota(jnp.int32, q.shape, 1)
    zero = jnp.zeros_like(q)
    return jnp.concatenate([jnp.where(lane < HEAD_DIM, q, zero),
                            jnp.where(lane >= HEAD_DIM, q, zero)], axis=0)


def _attn_finish(acc, l, lam, gs, tq):
    o = acc / l
    o = o[:tq] - lam * o[tq:]
    ms = jnp.mean(o * o, axis=-1, keepdims=True)
    return ((o * lax.rsqrt(ms + EPS)) * gs) * (1.0 - LAMBDA_INIT)


def _attn_prompt_kernel(lq_ref, lk_ref, gs_ref, q_ref, k_ref, v_ref, o_ref,
                        s_sc, p_sc, m_sc, l_sc, acc_sc, bias_sc, *, tile):
    i = pl.program_id(2)
    qq = _stack_maps(q_ref[0])
    lane_tiles = [slice(c * V_DIM, (c + 1) * V_DIM) for c in range(tile // V_DIM)]

    def kv_tile(ref, j):
        return ref[0, pl.ds(pl.multiple_of(j * tile, tile), tile), :]

    first_step = (pl.program_id(0) == 0) & (pl.program_id(1) == 0) & (i == 0)

    @pl.when(first_step)
    def _():
        r = lax.broadcasted_iota(jnp.int32, bias_sc.shape, 0)
        c = lax.broadcasted_iota(jnp.int32, bias_sc.shape, 1)
        r = jnp.where(r >= tile, r - tile, r)
        bias_sc[...] = jnp.where((c // CHUNK) <= (r // CHUNK), 0.0, NEG).astype(F32)

    def scores(j, masked):
        s = _dot_nt(qq, kv_tile(k_ref, j))
        s_sc[...] = s + bias_sc[...] if masked else s

    def prev_pv(j):
        return _dot(p_sc[...], kv_tile(v_ref, j))

    def softmax(pv):
        m_cur = s_sc[:, lane_tiles[0]]
        for lt in lane_tiles[1:]:
            m_cur = jnp.maximum(m_cur, s_sc[:, lt])
        m_old = m_sc[...]
        m_new = jnp.maximum(m_old, jnp.max(m_cur, axis=-1, keepdims=True))
        alpha = jnp.exp2(m_old - m_new)
        psum = None
        for lt in lane_tiles:
            pc = jnp.exp2(s_sc[:, lt] - m_new)
            p_sc[:, lt] = pc.astype(BF16)
            psum = pc if psum is None else psum + pc
        l_sc[...] = alpha * l_sc[...] + psum
        acc_sc[...] = alpha * (acc_sc[...] + pv)
        m_sc[...] = m_new

    def tile_of(k):
        return jnp.where(k == 0, i, k - 1)

    m_sc[...] = jnp.full(m_sc.shape, NEG, F32)
    l_sc[...] = jnp.zeros(l_sc.shape, F32)
    acc_sc[...] = jnp.zeros(acc_sc.shape, F32)
    p_sc[...] = jnp.zeros(p_sc.shape, BF16)
    scores(i, True)

    def body(k, carry):
        pv = prev_pv(tile_of(jnp.maximum(k - 1, 0)))
        softmax(pv)
        scores(jnp.minimum(k, jnp.maximum(i - 1, 0)), False)
        return carry

    lax.fori_loop(0, i + 1, body, 0)
    acc = acc_sc[...] + prev_pv(tile_of(i))
    l = jnp.sum(l_sc[...], axis=-1, keepdims=True)
    o_ref[0] = _attn_finish(acc, l, _lambda(lq_ref, lk_ref), gs_ref[...], tile).astype(BF16)


def _attn_prompt(lq, lk, gs, q, kb, vb):
    b, t, d = q.shape
    tile = min(t, ATTN_TILE)
    assert t % tile == 0 and tile % CHUNK == 0 and tile % V_DIM == 0
    qspec = pl.BlockSpec((1, tile, V_DIM), lambda bi, h, i: (bi, i, h))
    kspec = pl.BlockSpec((1, t, V_DIM), lambda bi, h, i: (bi, 0, h))
    rows = 2 * tile
    return pl.pallas_call(
        functools.partial(_attn_prompt_kernel, tile=tile),
        grid=(b, N_HEADS, t // tile),
        in_specs=[_const_spec((2, HEAD_DIM)), _const_spec((2, HEAD_DIM)), _const_spec((1, V_DIM)),
                  qspec, kspec, kspec],
        out_specs=qspec,
        out_shape=jax.ShapeDtypeStruct((b, t, d), BF16),
        scratch_shapes=[pltpu.VMEM((rows, tile), F32), pltpu.VMEM((rows, tile), BF16),
                        pltpu.VMEM((rows, V_DIM), F32), pltpu.VMEM((rows, V_DIM), F32),
                        pltpu.VMEM((rows, V_DIM), F32), pltpu.VMEM((rows, tile), F32)],
        compiler_params=_cparams(("arbitrary", "arbitrary", "arbitrary"), 40),
        name="attn_prompt",
    )(lq, lk, gs, q, kb, vb)


def _attn_sample_kernel(lq_ref, lk_ref, gs_ref, q_ref, ck_ref, cv_ref, kn_ref, vn_ref, o_ref):
    tq = q_ref.shape[1]
    qq = _stack_maps(q_ref[0])
    s_p = _dot_nt(qq, ck_ref[0].astype(BF16))
    s_n = _dot_nt(qq, kn_ref[0])
    m = jnp.maximum(jnp.max(s_p, axis=-1, keepdims=True), jnp.max(s_n, axis=-1, keepdims=True))
    p_p = jnp.exp2(s_p - m)
    p_n = jnp.exp2(s_n - m)
    l = jnp.sum(p_p, axis=-1, keepdims=True) + jnp.sum(p_n, axis=-1, keepdims=True)
    acc = _dot(p_p.astype(BF16), cv_ref[0].astype(BF16)) + _dot(p_n.astype(BF16), vn_ref[0])
    o_ref[0] = _attn_finish(acc, l, _lambda(lq_ref, lk_ref), gs_ref[...], tq).astype(BF16)


def _attn_sample(lq, lk, gs, q, cache_k, cache_v, kb, vb):
    b, t, d = q.shape
    past = cache_k.shape[1]
    qspec = pl.BlockSpec((1, t, V_DIM), lambda bi, h: (bi, 0, h))
    cspec = pl.BlockSpec((1, past, V_DIM), lambda bi, h: (bi, 0, h))
    return pl.pallas_call(
        _attn_sample_kernel,
        grid=(b, N_HEADS),
        in_specs=[_const_spec((2, HEAD_DIM)), _const_spec((2, HEAD_DIM)), _const_spec((1, V_DIM)),
                  qspec, cspec, cspec, qspec, qspec],
        out_specs=qspec,
        out_shape=jax.ShapeDtypeStruct((b, t, d), BF16),
        compiler_params=_cparams(("arbitrary", "arbitrary"), 32),
        name="attn_sample",
    )(lq, lk, gs, q, cache_k, cache_v, kb, vb)


def _cols4(cols, dtype):
    tm = cols[0].shape[0]
    lane = lax.broadcasted_iota(jnp.int32, (tm, TOP_K), 1)
    out = jnp.broadcast_to(cols[3], (tm, TOP_K))
    for kk in (2, 1, 0):
        out = jnp.where(lane == kk, cols[kk], out)
    return out.astype(dtype)


def _post_kernel(cnt_in_ref, x_ref, on_ref, sga_ref, gcc_ref, wap_ref, wout_ref, g_ref, wr_ref, br_ref,
                 h_ref, hn_ref, ti_ref, tw_ref, rk_ref, cnt_ref, carry):
    step = pl.program_id(0)
    tm = x_ref.shape[0]

    @pl.when(step == 0)
    def _():
        carry[...] = cnt_in_ref[...]

    a_out = _dot(on_ref[...], wap_ref[...])
    merged = sga_ref[...].astype(F32) * a_out + gcc_ref[...].astype(F32)
    h = x_ref[...] + _dot(merged.astype(BF16), wout_ref[...])
    h_ref[...] = h
    ms = jnp.mean(h * h, axis=-1, keepdims=True)
    hn32 = (h * lax.rsqrt(ms + EPS)) * g_ref[...]
    hn_ref[...] = _pack_halves(hn32)
    hn = hn32.astype(BF16)

    logits = _dot(hn, wr_ref[...]) + br_ref[...]
    lane = lax.broadcasted_iota(jnp.int32, logits.shape, 1)
    work = logits
    vals, idxs, sels = [], [], []
    for _ in range(TOP_K):
        mk = jnp.max(work, axis=-1, keepdims=True)
        ik = jnp.min(jnp.where(work == mk, lane, N_EXPERTS), axis=-1, keepdims=True)
        sel = lane == ik
        work = jnp.where(sel, -jnp.inf, work)
        vals.append(mk)
        idxs.append(ik)
        sels.append(sel)
    es = [jnp.exp(vk - vals[0]) for vk in vals]
    denom = es[0] + es[1] + es[2] + es[3]
    ti_ref[...] = _cols4(idxs, jnp.int32)
    tw_ref[...] = _cols4([e / denom for e in es], F32)

    picked = (sels[0] | sels[1] | sels[2] | sels[3])
    r = lax.broadcasted_iota(jnp.int32, (tm, tm), 0)
    c = lax.broadcasted_iota(jnp.int32, (tm, tm), 1)
    before = _dot((c < r).astype(BF16), picked.astype(BF16)) + carry[...]
    ranks = [jnp.sum(jnp.where(sk, before, 0.0), axis=-1, keepdims=True) for sk in sels]
    rk_ref[...] = _cols4(ranks, jnp.int32)
    carry[...] = carry[...] + jnp.sum(picked.astype(F32), axis=0, keepdims=True)
    cnt_ref[...] = carry[...]


def _post(cnt_in, x, on, sga, gcc, wap_b, wout_b, g, wr_b, br):
    n, d = x.shape
    tm = min(n, STREAM_TILE)
    assert n % tm == 0
    row = lambda: pl.BlockSpec((tm, d), lambda i: (i, 0))
    k4 = lambda: pl.BlockSpec((tm, TOP_K), lambda i: (i, 0))
    cnt = lambda: pl.BlockSpec((1, N_EXPERTS), lambda i: (0, 0))
    return pl.pallas_call(
        _post_kernel,
        grid=(n // tm,),
        in_specs=[cnt(), row(), row(), row(), row(),
                  _const_spec((d, d)), _const_spec((d, d)), _const_spec((1, d)),
                  _const_spec((d, N_EXPERTS)), _const_spec((1, N_EXPERTS))],
        out_specs=[row(), pl.BlockSpec((tm, d // 2), lambda i: (i, 0)), k4(), k4(), k4(), cnt()],
        out_shape=[jax.ShapeDtypeStruct((n, d), F32), jax.ShapeDtypeStruct((n, d // 2), jnp.int32),
                   jax.ShapeDtypeStruct((n, TOP_K), jnp.int32), jax.ShapeDtypeStruct((n, TOP_K), F32),
                   jax.ShapeDtypeStruct((n, TOP_K), jnp.int32),
                   jax.ShapeDtypeStruct((1, N_EXPERTS), F32)],
        scratch_shapes=[pltpu.VMEM((1, N_EXPERTS), F32)],
        compiler_params=_cparams(("arbitrary",), 32),
        name="post",
    )(cnt_in, x, on, sga, gcc, wap_b, wout_b, g, wr_b, br)


def _moe_kernel(te_ref, tv_ref, x_ref, wgu_ref, bg_ref, bu_ref, wd_ref, bd_ref, y_ref, wgu_sc, wd_sc):
    t = pl.program_id(0)
    valid = tv_ref[t] > 0
    new_expert = (t == 0) | (te_ref[t] != te_ref[jnp.maximum(t - 1, 0)])
    f = wd_ref.shape[1]
    pair = 2 * V_DIM

    @pl.when(valid & new_expert)
    def _():
        r = lax.broadcasted_iota(jnp.int32, (pair, pair), 0)
        c = lax.broadcasted_iota(jnp.int32, (pair, pair), 1)
        src = jnp.where(c < V_DIM, 2 * c, 2 * (c - V_DIM) + 1)
        perm = (r == src).astype(BF16)
        for b in range(2 * f // pair):
            blk = wgu_ref[0, :, b * pair:(b + 1) * pair].astype(BF16)
            wgu_sc[:, b * pair:(b + 1) * pair] = _dot(blk, perm).astype(BF16)
        wd_sc[...] = wd_ref[0].astype(BF16)

    @pl.when(valid)
    def _():
        x_lo, x_hi = _unpack_halves(x_ref[...])
        x = jnp.concatenate([x_lo.astype(BF16), x_hi.astype(BF16)], axis=1)
        hgu = _dot(x, wgu_sc[...])
        acts = []
        for b in range(f // V_DIM):
            cols = slice(b * V_DIM, (b + 1) * V_DIM)
            gate = jnp.minimum(hgu[:, b * pair:b * pair + V_DIM] + bg_ref[0, :, cols], SWIGLU_LIMIT)
            up = jnp.clip(hgu[:, b * pair + V_DIM:(b + 1) * pair] + bu_ref[0, :, cols],
                          -SWIGLU_LIMIT, SWIGLU_LIMIT)
            glu = gate * _sigmoid(SWIGLU_ALPHA * gate)
            acts.append((glu * (up + 1.0)).astype(BF16))
        y = _dot(jnp.concatenate(acts, axis=1), wd_sc[...]) + bd_ref[0]
        y_ref[...] = _pack_halves(y)


def _moe(tile_expert, tile_valid, xs, w_gu, bg, bu, w_down, bd):
    p = xs.shape[0]
    f, d = w_down.shape[1:]
    n_tiles = p // MOE_TILE
    row = pl.BlockSpec((MOE_TILE, d // 2), lambda i, te, tv: (i, 0))
    wspec = lambda a, b_: pl.BlockSpec((1, a, b_), lambda i, te, tv: (te[i], 0, 0))
    return pl.pallas_call(
        _moe_kernel,
        grid_spec=pltpu.PrefetchScalarGridSpec(
            num_scalar_prefetch=2,
            grid=(n_tiles,),
            in_specs=[row, wspec(d, 2 * f), wspec(1, f), wspec(1, f), wspec(f, d), wspec(1, d)],
            out_specs=row,
            scratch_shapes=[pltpu.VMEM((d, 2 * f), BF16), pltpu.VMEM((f, d), BF16)]),
        out_shape=jax.ShapeDtypeStruct((p, d // 2), jnp.int32),
        compiler_params=_cparams(("arbitrary",), 48),
        name="moe",
    )(tile_expert, tile_valid, xs, w_gu, bg, bu, w_down, bd)


def _sc_split(n):
    per_w = n // SC_WORKERS
    assert per_w * SC_WORKERS == n and per_w % 8 == 0, n
    ch = max(c for c in range(8, SC_MAX_CHUNK + 1, 8) if per_w % c == 0)
    return per_w, ch


def _sc_positions(pos):
    per_w, ch = _sc_split(pos.shape[0])
    return pos.reshape(SC_WORKERS, per_w // ch, ch, TOP_K).transpose(0, 1, 3, 2)


def _sc_mesh():
    return plsc.VectorSubcoreMesh(core_axis_name="c", subcore_axis_name="s",
                                  num_cores=SC_CORES, num_subcores=SC_SUBCORES)


def _sc_worker():
    return lax.axis_index("s") * SC_CORES + lax.axis_index("c")


def _sc_dispatch(srcs, poss, p_rows):
    w = srcs[0].shape[1]
    geo = [_sc_split(s.shape[0]) for s in srcs]
    nseg = len(srcs)
    scratch = []
    for _, ch in geo:
        scratch += [pltpu.VMEM((TOP_K, ch), jnp.int32), pltpu.VMEM((ch, w), jnp.int32)]
    scratch.append(pltpu.SemaphoreType.DMA)

    def body(*refs):
        src_refs, pos_refs, xs_ref = refs[:nseg], refs[nseg:2 * nseg], refs[2 * nseg]
        scr = refs[2 * nseg + 1:]
        sem = scr[-1]
        wid = _sc_worker()
        for g in range(nseg):
            per_w, ch = geo[g]

            @pl.loop(0, per_w // ch)
            def _(c, src=src_refs[g], pos=pos_refs[g], idx_v=scr[2 * g], rows_v=scr[2 * g + 1],
                  per_w=per_w, ch=ch):
                base = pl.multiple_of(wid * per_w + c * ch, 8)
                pltpu.sync_copy(pos.at[wid, c], idx_v)
                pltpu.sync_copy(src.at[pl.ds(base, ch)], rows_v)
                copies = [pltpu.async_copy(rows_v, xs_ref.at[idx_v.at[kk]], sem) for kk in range(TOP_K)]
                for cp in copies:
                    cp.wait()

    return pl.kernel(body, out_type=jax.ShapeDtypeStruct((p_rows, w), jnp.int32), mesh=_sc_mesh(),
                     scratch_types=scratch, name="sc_dispatch")(*srcs, *poss)


def _sc_gather(ys, poss, seg_rows):
    w = ys.shape[1]
    geo = [_sc_split(n) for n in seg_rows]
    nseg = len(seg_rows)
    row0 = [sum(seg_rows[:g]) for g in range(nseg)]
    scratch = []
    for _, ch in geo:
        scratch += [pltpu.VMEM((TOP_K, ch), jnp.int32), pltpu.VMEM((TOP_K, ch, w), jnp.int32)]
    scratch += [pltpu.SemaphoreType.DMA, pltpu.SemaphoreType.DMA]

    def body(*refs):
        ys_ref, pos_refs, out_ref = refs[0], refs[1:1 + nseg], refs[1 + nseg]
        scr = refs[2 + nseg:]
        sem_g, sem_w = scr[-2], scr[-1]
        wid = _sc_worker()
        for g in range(nseg):
            per_w, ch = geo[g]

            @pl.loop(0, per_w // ch)
            def _(c, pos=pos_refs[g], idx_v=scr[2 * g], rows_v=scr[2 * g + 1], per_w=per_w, ch=ch,
                  r0=row0[g]):
                base = pl.multiple_of(r0 + wid * per_w + c * ch, 8)
                pltpu.sync_copy(pos.at[wid, c], idx_v)
                gathers = [pltpu.async_copy(ys_ref.at[idx_v.at[kk]], rows_v.at[kk], sem_g)
                           for kk in range(TOP_K)]
                writes = []
                for kk in range(TOP_K):
                    gathers[kk].wait()
                    writes.append(pltpu.async_copy(rows_v.at[kk], out_ref.at[kk, pl.ds(base, ch)], sem_w))
                for cp in writes:
                    cp.wait()

    return pl.kernel(body, out_type=jax.ShapeDtypeStruct((TOP_K, sum(seg_rows), w), jnp.int32),
                     mesh=_sc_mesh(), scratch_types=scratch, name="sc_gather")(ys, *poss)


def _final_kernel(h_ref, yg_ref, tw_ref, g_ref, o_ref):
    tw = tw_ref[...]
    acc = None
    for kk in range(TOP_K):
        lo, hi = _unpack_halves(yg_ref[kk])
        part = jnp.concatenate([lo, hi], axis=1) * tw[:, kk:kk + 1]
        acc = part if acc is None else acc + part
    h = h_ref[...] + acc
    ms = jnp.mean(h * h, axis=-1, keepdims=True)
    o_ref[...] = (h * lax.rsqrt(ms + EPS)) * g_ref[...]


def _final(h, yg, tw, g, row0):
    n, d = h.shape
    tm = min(n, STREAM_TILE)
    assert n % tm == 0 and row0 % tm == 0
    blk0 = row0 // tm
    return pl.pallas_call(
        _final_kernel,
        grid=(n // tm,),
        in_specs=[pl.BlockSpec((tm, d), lambda i: (i, 0)),
                  pl.BlockSpec((TOP_K, tm, d // 2), lambda i: (0, i + blk0, 0)),
                  pl.BlockSpec((tm, TOP_K), lambda i: (i, 0)),
                  _const_spec((1, d))],
        out_specs=pl.BlockSpec((tm, d), lambda i: (i, 0)),
        out_shape=jax.ShapeDtypeStruct((n, d), F32),
        compiler_params=_cparams(("arbitrary",), 32),
        name="final",
    )(h, yg, tw, g)


def kernel(x_prompt, x_sample, cache_k, cache_v, state_conv, g_attn_norm, w_in, lambda_q, lambda_k,
           g_subln, conv_w, w_attn_proj, w_conv_proj, w_out, g_ffn_norm, w_router, b_router, w_gu, b_gu,
           w_down, b_down, g_final):
    assert w_in.shape[0] == 1, "single-layer trunk"
    bp, tp, d = x_prompt.shape
    bs, ts, _ = x_sample.shape
    past = cache_k.shape[2]
    f = w_down.shape[2]
    n_p, n_s = bp * tp, bs * ts
    n_all = n_p + n_s

    w_in_b = w_in[0].astype(BF16)
    wcp_b = w_conv_proj[0].astype(BF16)
    wap_b = w_attn_proj[0].astype(BF16)
    wout_b = w_out[0].astype(BF16)
    wr_b = w_router[0].astype(BF16)
    bgu = b_gu[0].reshape(N_EXPERTS, 1, f, 2)
    bg, bu = bgu[..., 0], bgu[..., 1]
    bd = b_down[0].reshape(N_EXPERTS, 1, d)
    g_attn = g_attn_norm[0].reshape(1, d)
    g_ffn = g_ffn_norm[0].reshape(1, d)
    gs = g_subln[0].reshape(1, V_DIM)
    lq, lk = lambda_q[0], lambda_k[0]
    br = b_router[0].reshape(1, N_EXPERTS)

    prev_p = jnp.zeros((bp, 8, d), F32)
    prev_s = jnp.pad(state_conv[0], ((0, 0), (6, 0), (0, 0)))

    q_p, k_p, v_p, kb_p, vb_p, sga_p, gcc_p, conv_p = _inproj(x_prompt, prev_p, g_attn, w_in_b, conv_w[0], wcp_b)
    q_s, k_s, v_s, kb_s, vb_s, sga_s, gcc_s, conv_s = _inproj(x_sample, prev_s, g_attn, w_in_b, conv_w[0], wcp_b)

    on_p = _attn_prompt(lq, lk, gs, q_p, kb_p, vb_p)
    on_s = _attn_sample(lq, lk, gs, q_s, cache_k[0].reshape(bs, past, d), cache_v[0].reshape(bs, past, d),
                        kb_s, vb_s)

    flat = lambda a, n: a.reshape(n, d)
    cnt0 = jnp.zeros((1, N_EXPERTS), F32)
    h_p, hn_p, ti_p, tw_p, rk_p, cnt_p = _post(cnt0, flat(x_prompt, n_p), flat(on_p, n_p), flat(sga_p, n_p),
                                               flat(gcc_p, n_p), wap_b, wout_b, g_ffn, wr_b, br)
    h_s, hn_s, ti_s, tw_s, rk_s, cnt = _post(cnt_p, flat(x_sample, n_s), flat(on_s, n_s), flat(sga_s, n_s),
                                             flat(gcc_s, n_s), wap_b, wout_b, g_ffn, wr_b, br)

    counts = cnt[0].astype(jnp.int32)
    padded = ((counts + MOE_TILE - 1) // MOE_TILE) * MOE_TILE
    ends = jnp.cumsum(padded)
    starts = ends - padded
    n_tiles = (n_all * TOP_K) // MOE_TILE + N_EXPERTS
    tile_row = jnp.arange(n_tiles, dtype=jnp.int32) * MOE_TILE
    tile_expert = jnp.sum((tile_row[:, None] >= ends[None, :]).astype(jnp.int32), axis=1)
    tile_expert = jnp.minimum(tile_expert, N_EXPERTS - 1)
    tile_valid = (tile_row < ends[-1]).astype(jnp.int32)

    poss = [_sc_positions(starts[ti_p] + rk_p), _sc_positions(starts[ti_s] + rk_s)]
    xs = _sc_dispatch([hn_p, hn_s], poss, n_tiles * MOE_TILE)
    ys = _moe(tile_expert, tile_valid, xs, w_gu[0], bg, bu, w_down[0], bd)
    yg = _sc_gather(ys, poss, [n_p, n_s])

    y_p = _final(h_p, yg, tw_p, g_final.reshape(1, d), 0)
    y_s = _final(h_s, yg, tw_s, g_final.reshape(1, d), n_p)

    hd = N_HEADS
    return (y_p.reshape(bp, tp, d), y_s.reshape(bs, ts, d),
            k_p.reshape(1, bp, tp, hd, V_DIM), v_p.reshape(1, bp, tp, hd, V_DIM), conv_p[None],
            k_s.reshape(1, bs, ts, hd, V_DIM), v_s.reshape(1, bs, ts, hd, V_DIM), conv_s[None])
```

```python
import functools
import math

import jax
import jax.numpy as jnp
from jax import lax
from jax.experimental import pallas as pl
from jax.experimental.pallas import tpu as pltpu
from jax.experimental.pallas import tpu_sc as plsc

N_HEADS = 8
HEAD_DIM = 64
V_DIM = 2 * HEAD_DIM
CHUNK = 64
N_EXPERTS = 32
TOP_K = 4
SWIGLU_LIMIT = 7.0
SWIGLU_ALPHA = 1.702
EPS = 1e-6
NEG = -1e30
LAMBDA_INIT = 0.8 - 0.6 * math.exp(-0.3 * 0)
N_SEG = 8
Q_SCALE = HEAD_DIM ** -0.5 * math.log2(math.e)

ROW_TILE = 256
STREAM_TILE = 512
ATTN_TILE = 1024
MOE_TILE = 512
MIB = 1024 * 1024
SC_CORES = 2
SC_SUBCORES = 16
SC_WORKERS = SC_CORES * SC_SUBCORES
SC_MAX_CHUNK = 32

BF16 = jnp.bfloat16
F32 = jnp.float32


def _cparams(semantics, vmem_mib):
    return pltpu.CompilerParams(dimension_semantics=semantics, vmem_limit_bytes=vmem_mib * MIB)


def _const_spec(shape):
    nd = len(shape)
    return pl.BlockSpec(shape, lambda *_: (0,) * nd, pipeline_mode=pl.Buffered(1))


def _sigmoid(x):
    return 1.0 / (1.0 + jnp.exp(-x))


def _dot(a, b):
    return jnp.dot(a, b, preferred_element_type=F32)


def _dot_nt(a, b):
    return lax.dot_general(a, b, (((1,), (1,)), ((), ())), preferred_element_type=F32)


def _pack_halves(x):
    w = x.shape[1] // 2
    lo = lax.bitcast_convert_type(x[:, :w].astype(BF16).astype(F32), jnp.uint32)
    hi = lax.bitcast_convert_type(x[:, w:].astype(BF16).astype(F32), jnp.uint32)
    return lax.bitcast_convert_type((lo >> 16) | (hi & jnp.uint32(0xFFFF0000)), jnp.int32)


def _unpack_halves(words):
    u = lax.bitcast_convert_type(words, jnp.uint32)
    lo = lax.bitcast_convert_type(u << 16, F32)
    hi = lax.bitcast_convert_type(u & jnp.uint32(0xFFFF0000), F32)
    return lo, hi


def _inproj_kernel(x_ref, prev_ref, g_ref, w_ref, cw_ref, wcp_ref,
                   q_ref, k32_ref, v32_ref, kb_ref, vb_ref, sga_ref, gcc_ref, cnew_ref, ubuf):
    t = pl.program_id(1)
    tm, d = x_ref.shape[1], x_ref.shape[2]

    @pl.when(t == 0)
    def _():
        ubuf[0:8, :] = prev_ref[0]

    x = x_ref[0]
    ms = jnp.mean(x * x, axis=-1, keepdims=True)
    xn = ((x * lax.rsqrt(ms + EPS)) * g_ref[...]).astype(BF16)

    def seg(i):
        return _dot(xn, w_ref[:, i * d:(i + 1) * d])

    q_ref[0] = (seg(0) * Q_SCALE).astype(BF16)
    k = seg(1)
    k32_ref[0] = k
    kb_ref[0] = k.astype(BF16)
    v = seg(2)
    v32_ref[0] = v
    vb_ref[0] = v.astype(BF16)

    u = seg(5) * seg(3)
    ubuf[8:8 + tm, :] = u
    conv = (ubuf[6:6 + tm, :] * cw_ref[0:1, :] + ubuf[7:7 + tm, :] * cw_ref[1:2, :]
            + u * cw_ref[2:3, :])
    cpre = (seg(4) * conv).astype(BF16)
    c_out = _dot(cpre, wcp_ref[...])
    gcc_ref[0] = (_sigmoid(seg(7)) * c_out).astype(BF16)
    sga_ref[0] = _sigmoid(seg(6)).astype(BF16)
    cnew_ref[0] = ubuf[tm + 6:tm + 8, :]
    ubuf[0:8, :] = ubuf[tm:tm + 8, :]


def _inproj(x, conv_prev8, g, w_in_b, conv_w, wcp_b):
    b, t, d = x.shape
    tm = min(t, ROW_TILE)
    assert t % tm == 0 and tm % 8 == 0
    row = lambda: pl.BlockSpec((1, tm, d), lambda i, j: (i, j, 0))
    bf = jax.ShapeDtypeStruct((b, t, d), BF16)
    f32 = jax.ShapeDtypeStruct((b, t, d), F32)
    return pl.pallas_call(
        _inproj_kernel,
        grid=(b, t // tm),
        in_specs=[row(),
                  pl.BlockSpec((1, 8, d), lambda i, j: (i, 0, 0)),
                  _const_spec((1, d)),
                  _const_spec((d, N_SEG * d)),
                  _const_spec((3, d)),
                  _const_spec((d, d))],
        out_specs=[row(), row(), row(), row(), row(), row(), row(),
                   pl.BlockSpec((1, 2, d), lambda i, j: (i, 0, 0))],
        out_shape=[bf, f32, f32, bf, bf, bf, bf, jax.ShapeDtypeStruct((b, 2, d), F32)],
        scratch_shapes=[pltpu.VMEM((tm + 8, d), F32)],
        compiler_params=_cparams(("arbitrary", "arbitrary"), 56),
        name="inproj",
    )(x, conv_prev8, g, w_in_b, conv_w, wcp_b)


def _lambda(lq_ref, lk_ref):
    e = jnp.exp(jnp.sum(lq_ref[...] * lk_ref[...], axis=-1, keepdims=True))
    return e[0:1, :] - e[1:2, :] + LAMBDA_INIT


def _stack_maps(q):
    lane = lax.broadcasted_iota(jnp.int32, q.shape, 1)
    zero = jnp.zeros_like(q)
    return jnp.concatenate([jnp.where(lane < HEAD_DIM, q, zero),
                            jnp.where(lane >= HEAD_DIM, q, zero)], axis=0)


def _attn_finish(acc, l, lam, gs, tq):
    o = acc / l
    o = o[:tq] - lam * o[tq:]
    ms = jnp.mean(o * o, axis=-1, keepdims=True)
    return ((o * lax.rsqrt(ms + EPS)) * gs) * (1.0 - LAMBDA_INIT)


def _attn_prompt_kernel(lq_ref, lk_ref, gs_ref, q_ref, k_ref, v_ref, o_ref,
                        s_sc, p_sc, qq_sc, m_sc, l_sc, acc_sc, bias_sc, *, tile):
    n_q = q_ref.shape[1] // tile
    lane_tiles = [slice(c * V_DIM, (c + 1) * V_DIM) for c in range(tile // V_DIM)]
    lam = _lambda(lq_ref, lk_ref)

    def tile_rows(j):
        return pl.ds(j * tile if isinstance(j, int) else pl.multiple_of(j * tile, tile), tile)

    def rows_of(ref, j):
        return ref[0, tile_rows(j), :]

    @pl.when((pl.program_id(0) == 0) & (pl.program_id(1) == 0))
    def _():
        r = lax.broadcasted_iota(jnp.int32, bias_sc.shape, 0)
        c = lax.broadcasted_iota(jnp.int32, bias_sc.shape, 1)
        r = jnp.where(r >= tile, r - tile, r)
        bias_sc[...] = jnp.where((c // CHUNK) <= (r // CHUNK), 0.0, NEG).astype(F32)

    def key_tile(j, p):
        if isinstance(p, int):
            return j if p == 0 else p - 1
        return jnp.where(p == 0, j, p - 1)

    def diag_scores(j):
        qq = _stack_maps(rows_of(q_ref, j))
        qq_sc[...] = qq
        s_sc[...] = _dot_nt(qq, rows_of(k_ref, j)) + bias_sc[...]

    def scores(kt):
        s_sc[...] = _dot_nt(qq_sc[...], rows_of(k_ref, kt))

    def probs_v(kt):
        return _dot(p_sc[...], rows_of(v_ref, kt))

    def softmax(pv):
        m_cur = s_sc[:, lane_tiles[0]]
        for lt in lane_tiles[1:]:
            m_cur = jnp.maximum(m_cur, s_sc[:, lt])
        m_cur = jnp.max(m_cur, axis=-1, keepdims=True)
        if pv is None:
            m_new = jnp.broadcast_to(m_cur, m_sc.shape)
        else:
            m_old = m_sc[...]
            m_new = jnp.maximum(m_old, m_cur)
            alpha = jnp.exp2(m_old - m_new)
        psum = None
        for lt in lane_tiles:
            pc = jnp.exp2(s_sc[:, lt] - m_new)
            p_sc[:, lt] = pc.astype(BF16)
            psum = pc if psum is None else psum + pc
        if pv is None:
            l_sc[...] = psum
            acc_sc[...] = jnp.zeros(acc_sc.shape, F32)
        else:
            l_sc[...] = alpha * l_sc[...] + psum
            acc_sc[...] = alpha * (acc_sc[...] + pv)
        m_sc[...] = m_new

    def finish(j, last_kt):
        acc = acc_sc[...] + probs_v(last_kt)
        l = jnp.sum(l_sc[...], axis=-1, keepdims=True)
        o_ref[0, tile_rows(j), :] = _attn_finish(
            acc, l, lam, gs_ref[...], tile).astype(BF16)

    diag_scores(0)
    softmax(None)
    diag_scores(min(1, n_q - 1))

    def query_tile(j, carry):
        finish(j - 1, key_tile(j - 1, j - 1))
        softmax(None)
        scores(0)

        def body(p, c):
            pv = probs_v(key_tile(j, p - 1))
            softmax(pv)
            scores(p)
            return c

        lax.fori_loop(1, j, body, 0)
        pv = probs_v(key_tile(j, j - 1))
        softmax(pv)
        diag_scores(jnp.minimum(j + 1, n_q - 1))
        return carry

    lax.fori_loop(1, n_q, query_tile, 0)
    finish(n_q - 1, key_tile(n_q - 1, n_q - 1))


def _attn_vmem_mib(t, tile):
    rows = 2 * tile
    score_f32 = rows * tile * 4
    scratch = (2 * score_f32 + score_f32 // 2 + 3 * rows * V_DIM * 4
               + rows * V_DIM * 2)
    blocks = 2 * 4 * (t * V_DIM * 2)
    temporaries = score_f32
    return -(-(scratch + blocks + temporaries) // MIB) + 6


def _attn_prompt(lq, lk, gs, q, kb, vb):
    b, t, d = q.shape
    tile = min(t, ATTN_TILE)
    assert t % tile == 0 and tile % CHUNK == 0 and tile % V_DIM == 0
    spec = pl.BlockSpec((1, t, V_DIM), lambda bi, h: (bi, 0, h))
    rows = 2 * tile
    return pl.pallas_call(
        functools.partial(_attn_prompt_kernel, tile=tile),
        grid=(b, N_HEADS),
        in_specs=[_const_spec((2, HEAD_DIM)), _const_spec((2, HEAD_DIM)), _const_spec((1, V_DIM)),
                  spec, spec, spec],
        out_specs=spec,
        out_shape=jax.ShapeDtypeStruct((b, t, d), BF16),
        scratch_shapes=[pltpu.VMEM((rows, tile), F32), pltpu.VMEM((rows, tile), BF16),
                        pltpu.VMEM((rows, V_DIM), BF16),
                        pltpu.VMEM((rows, V_DIM), F32), pltpu.VMEM((rows, V_DIM), F32),
                        pltpu.VMEM((rows, V_DIM), F32), pltpu.VMEM((rows, tile), F32)],
        compiler_params=_cparams(("arbitrary", "arbitrary"), _attn_vmem_mib(t, tile)),
        name="attn_prompt",
    )(lq, lk, gs, q, kb, vb)


def _attn_sample_kernel(lq_ref, lk_ref, gs_ref, q_ref, ck_ref, cv_ref, kn_ref, vn_ref, o_ref):
    tq = q_ref.shape[1]
    qq = _stack_maps(q_ref[0])
    s_p = _dot_nt(qq, ck_ref[0].astype(BF16))
    s_n = _dot_nt(qq, kn_ref[0])
    m = jnp.maximum(jnp.max(s_p, axis=-1, keepdims=True), jnp.max(s_n, axis=-1, keepdims=True))
    p_p = jnp.exp2(s_p - m)
    p_n = jnp.exp2(s_n - m)
    l = jnp.sum(p_p, axis=-1, keepdims=True) + jnp.sum(p_n, axis=-1, keepdims=True)
    acc = _dot(p_p.astype(BF16), cv_ref[0].astype(BF16)) + _dot(p_n.astype(BF16), vn_ref[0])
    o_ref[0] = _attn_finish(acc, l, _lambda(lq_ref, lk_ref), gs_ref[...], tq).astype(BF16)


def _attn_sample(lq, lk, gs, q, cache_k, cache_v, kb, vb):
    b, t, d = q.shape
    past = cache_k.shape[1]
    qspec = pl.BlockSpec((1, t, V_DIM), lambda bi, h: (bi, 0, h))
    cspec = pl.BlockSpec((1, past, V_DIM), lambda bi, h: (bi, 0, h))
    return pl.pallas_call(
        _attn_sample_kernel,
        grid=(b, N_HEADS),
        in_specs=[_const_spec((2, HEAD_DIM)), _const_spec((2, HEAD_DIM)), _const_spec((1, V_DIM)),
                  qspec, cspec, cspec, qspec, qspec],
        out_specs=qspec,
        out_shape=jax.ShapeDtypeStruct((b, t, d), BF16),
        compiler_params=_cparams(("arbitrary", "arbitrary"), 32),
        name="attn_sample",
    )(lq, lk, gs, q, cache_k, cache_v, kb, vb)


def _cols4(cols, dtype):
    tm = cols[0].shape[0]
    lane = lax.broadcasted_iota(jnp.int32, (tm, TOP_K), 1)
    out = jnp.broadcast_to(cols[3], (tm, TOP_K))
    for kk in (2, 1, 0):
        out = jnp.where(lane == kk, cols[kk], out)
    return out.astype(dtype)


def _post_kernel(cnt_in_ref, x_ref, on_ref, sga_ref, gcc_ref, wap_ref, wout_ref, g_ref, wr_ref, br_ref,
                 h_ref, hn_ref, ti_ref, tw_ref, rk_ref, cnt_ref, carry):
    step = pl.program_id(0)
    tm = x_ref.shape[0]

    @pl.when(step == 0)
    def _():
        carry[...] = cnt_in_ref[...]

    a_out = _dot(on_ref[...], wap_ref[...])
    merged = sga_ref[...].astype(F32) * a_out + gcc_ref[...].astype(F32)
    h = x_ref[...] + _dot(merged.astype(BF16), wout_ref[...])
    h_ref[...] = h
    ms = jnp.mean(h * h, axis=-1, keepdims=True)
    hn32 = (h * lax.rsqrt(ms + EPS)) * g_ref[...]
    hn_ref[...] = _pack_halves(hn32)
    hn = hn32.astype(BF16)

    logits = _dot(hn, wr_ref[...]) + br_ref[...]
    lane = lax.broadcasted_iota(jnp.int32, logits.shape, 1)
    work = logits
    vals, idxs, sels = [], [], []
    for _ in range(TOP_K):
        mk = jnp.max(work, axis=-1, keepdims=True)
        ik = jnp.min(jnp.where(work == mk, lane, N_EXPERTS), axis=-1, keepdims=True)
        sel = lane == ik
        work = jnp.where(sel, -jnp.inf, work)
        vals.append(mk)
        idxs.append(ik)
        sels.append(sel)
    es = [jnp.exp(vk - vals[0]) for vk in vals]
    denom = es[0] + es[1] + es[2] + es[3]
    ti_ref[...] = _cols4(idxs, jnp.int32)
    tw_ref[...] = _cols4([e / denom for e in es], F32)

    picked = (sels[0] | sels[1] | sels[2] | sels[3])
    r = lax.broadcasted_iota(jnp.int32, (tm, tm), 0)
    c = lax.broadcasted_iota(jnp.int32, (tm, tm), 1)
    before = _dot((c < r).astype(BF16), picked.astype(BF16)) + carry[...]
    ranks = [jnp.sum(jnp.where(sk, before, 0.0), axis=-1, keepdims=True) for sk in sels]
    rk_ref[...] = _cols4(ranks, jnp.int32)
    carry[...] = carry[...] + jnp.sum(picked.astype(F32), axis=0, keepdims=True)
    cnt_ref[...] = carry[...]


def _post(cnt_in, x, on, sga, gcc, wap_b, wout_b, g, wr_b, br):
    n, d = x.shape
    tm = min(n, STREAM_TILE)
    assert n % tm == 0
    row = lambda: pl.BlockSpec((tm, d), lambda i: (i, 0))
    k4 = lambda: pl.BlockSpec((tm, TOP_K), lambda i: (i, 0))
    cnt = lambda: pl.BlockSpec((1, N_EXPERTS), lambda i: (0, 0))
    return pl.pallas_call(
        _post_kernel,
        grid=(n // tm,),
        in_specs=[cnt(), row(), row(), row(), row(),
                  _const_spec((d, d)), _const_spec((d, d)), _const_spec((1, d)),
                  _const_spec((d, N_EXPERTS)), _const_spec((1, N_EXPERTS))],
        out_specs=[row(), pl.BlockSpec((tm, d // 2), lambda i: (i, 0)), k4(), k4(), k4(), cnt()],
        out_shape=[jax.ShapeDtypeStruct((n, d), F32), jax.ShapeDtypeStruct((n, d // 2), jnp.int32),
                   jax.ShapeDtypeStruct((n, TOP_K), jnp.int32), jax.ShapeDtypeStruct((n, TOP_K), F32),
                   jax.ShapeDtypeStruct((n, TOP_K), jnp.int32),
                   jax.ShapeDtypeStruct((1, N_EXPERTS), F32)],
        scratch_shapes=[pltpu.VMEM((1, N_EXPERTS), F32)],
        compiler_params=_cparams(("arbitrary",), 32),
        name="post",
    )(cnt_in, x, on, sga, gcc, wap_b, wout_b, g, wr_b, br)


def _moe_kernel(te_ref, tv_ref, x_ref, wgu_ref, bg_ref, bu_ref, wd_ref, bd_ref, y_ref, wgu_sc, wd_sc):
    t = pl.program_id(0)
    valid = tv_ref[t] > 0
    new_expert = (t == 0) | (te_ref[t] != te_ref[jnp.maximum(t - 1, 0)])
    f = wd_ref.shape[1]
    pair = 2 * V_DIM

    @pl.when(valid & new_expert)
    def _():
        r = lax.broadcasted_iota(jnp.int32, (pair, pair), 0)
        c = lax.broadcasted_iota(jnp.int32, (pair, pair), 1)
        src = jnp.where(c < V_DIM, 2 * c, 2 * (c - V_DIM) + 1)
        perm = (r == src).astype(BF16)
        for b in range(2 * f // pair):
            blk = wgu_ref[0, :, b * pair:(b + 1) * pair].astype(BF16)
            wgu_sc[:, b * pair:(b + 1) * pair] = _dot(blk, perm).astype(BF16)
        wd_sc[...] = wd_ref[0].astype(BF16)

    @pl.when(valid)
    def _():
        x_lo, x_hi = _unpack_halves(x_ref[...])
        x = jnp.concatenate([x_lo.astype(BF16), x_hi.astype(BF16)], axis=1)
        hgu = _dot(x, wgu_sc[...])
        acts = []
        for b in range(f // V_DIM):
            cols = slice(b * V_DIM, (b + 1) * V_DIM)
            gate = jnp.minimum(hgu[:, b * pair:b * pair + V_DIM] + bg_ref[0, :, cols], SWIGLU_LIMIT)
            up = jnp.clip(hgu[:, b * pair + V_DIM:(b + 1) * pair] + bu_ref[0, :, cols],
                          -SWIGLU_LIMIT, SWIGLU_LIMIT)
            glu = gate * _sigmoid(SWIGLU_ALPHA * gate)
            acts.append((glu * (up + 1.0)).astype(BF16))
        y = _dot(jnp.concatenate(acts, axis=1), wd_sc[...]) + bd_ref[0]
        y_ref[...] = _pack_halves(y)


def _moe(tile_expert, tile_valid, xs, w_gu, bg, bu, w_down, bd):
    p = xs.shape[0]
    f, d = w_down.shape[1:]
    n_tiles = p // MOE_TILE
    row = pl.BlockSpec((MOE_TILE, d // 2), lambda i, te, tv: (i, 0))
    wspec = lambda a, b_: pl.BlockSpec((1, a, b_), lambda i, te, tv: (te[i], 0, 0))
    return pl.pallas_call(
        _moe_kernel,
        grid_spec=pltpu.PrefetchScalarGridSpec(
            num_scalar_prefetch=2,
            grid=(n_tiles,),
            in_specs=[row, wspec(d, 2 * f), wspec(1, f), wspec(1, f), wspec(f, d), wspec(1, d)],
            out_specs=row,
            scratch_shapes=[pltpu.VMEM((d, 2 * f), BF16), pltpu.VMEM((f, d), BF16)]),
        out_shape=jax.ShapeDtypeStruct((p, d // 2), jnp.int32),
        compiler_params=_cparams(("arbitrary",), 48),
        name="moe",
    )(tile_expert, tile_valid, xs, w_gu, bg, bu, w_down, bd)


def _sc_split(n):
    per_w = n // SC_WORKERS
    assert per_w * SC_WORKERS == n and per_w % 8 == 0, n
    ch = max(c for c in range(8, SC_MAX_CHUNK + 1, 8) if per_w % c == 0)
    return per_w, ch


def _sc_positions(pos):
    per_w, ch = _sc_split(pos.shape[0])
    return pos.reshape(SC_WORKERS, per_w // ch, ch, TOP_K).transpose(0, 1, 3, 2)


def _sc_mesh():
    return plsc.VectorSubcoreMesh(core_axis_name="c", subcore_axis_name="s",
                                  num_cores=SC_CORES, num_subcores=SC_SUBCORES)


def _sc_worker():
    return lax.axis_index("s") * SC_CORES + lax.axis_index("c")


def _sc_dispatch(srcs, poss, p_rows):
    w = srcs[0].shape[1]
    geo = [_sc_split(s.shape[0]) for s in srcs]
    nseg = len(srcs)
    scratch = []
    for _, ch in geo:
        scratch += [pltpu.VMEM((TOP_K, ch), jnp.int32), pltpu.VMEM((ch, w), jnp.int32)]
    scratch.append(pltpu.SemaphoreType.DMA)

    def body(*refs):
        src_refs, pos_refs, xs_ref = refs[:nseg], refs[nseg:2 * nseg], refs[2 * nseg]
        scr = refs[2 * nseg + 1:]
        sem = scr[-1]
        wid = _sc_worker()
        for g in range(nseg):
            per_w, ch = geo[g]

            @pl.loop(0, per_w // ch)
            def _(c, src=src_refs[g], pos=pos_refs[g], idx_v=scr[2 * g], rows_v=scr[2 * g + 1],
                  per_w=per_w, ch=ch):
                base = pl.multiple_of(wid * per_w + c * ch, 8)
                pltpu.sync_copy(pos.at[wid, c], idx_v)
                pltpu.sync_copy(src.at[pl.ds(base, ch)], rows_v)
                copies = [pltpu.async_copy(rows_v, xs_ref.at[idx_v.at[kk]], sem) for kk in range(TOP_K)]
                for cp in copies:
                    cp.wait()

    return pl.kernel(body, out_type=jax.ShapeDtypeStruct((p_rows, w), jnp.int32), mesh=_sc_mesh(),
                     scratch_types=scratch, name="sc_dispatch")(*srcs, *poss)


def _sc_gather(ys, poss, seg_rows):
    w = ys.shape[1]
    geo = [_sc_split(n) for n in seg_rows]
    nseg = len(seg_rows)
    row0 = [sum(seg_rows[:g]) for g in range(nseg)]
    scratch = []
    for _, ch in geo:
        scratch += [pltpu.VMEM((TOP_K, ch), jnp.int32), pltpu.VMEM((TOP_K, ch, w), jnp.int32)]
    scratch += [pltpu.SemaphoreType.DMA, pltpu.SemaphoreType.DMA]

    def body(*refs):
        ys_ref, pos_refs, out_ref = refs[0], refs[1:1 + nseg], refs[1 + nseg]
        scr = refs[2 + nseg:]
        sem_g, sem_w = scr[-2], scr[-1]
        wid = _sc_worker()
        for g in range(nseg):
            per_w, ch = geo[g]

            @pl.loop(0, per_w // ch)
            def _(c, pos=pos_refs[g], idx_v=scr[2 * g], rows_v=scr[2 * g + 1], per_w=per_w, ch=ch,
                  r0=row0[g]):
                base = pl.multiple_of(r0 + wid * per_w + c * ch, 8)
                pltpu.sync_copy(pos.at[wid, c], idx_v)
                gathers = [pltpu.async_copy(ys_ref.at[idx_v.at[kk]], rows_v.at[kk], sem_g)
                           for kk in range(TOP_K)]
                writes = []
                for kk in range(TOP_K):
                    gathers[kk].wait()
                    writes.append(pltpu.async_copy(rows_v.at[kk], out_ref.at[kk, pl.ds(base, ch)], sem_w))
                for cp in writes:
                    cp.wait()

    return pl.kernel(body, out_type=jax.ShapeDtypeStruct((TOP_K, sum(seg_rows), w), jnp.int32),
                     mesh=_sc_mesh(), scratch_types=scratch, name="sc_gather")(ys, *poss)


def _final_kernel(h_ref, yg_ref, tw_ref, g_ref, o_ref):
    tw = tw_ref[...]
    acc = None
    for kk in range(TOP_K):
        lo, hi = _unpack_halves(yg_ref[kk])
        part = jnp.concatenate([lo, hi], axis=1) * tw[:, kk:kk + 1]
        acc = part if acc is None else acc + part
    h = h_ref[...] + acc
    ms = jnp.mean(h * h, axis=-1, keepdims=True)
    o_ref[...] = (h * lax.rsqrt(ms + EPS)) * g_ref[...]


def _final(h, yg, tw, g, row0):
    n, d = h.shape
    tm = min(n, STREAM_TILE)
    assert n % tm == 0 and row0 % tm == 0
    blk0 = row0 // tm
    return pl.pallas_call(
        _final_kernel,
        grid=(n // tm,),
        in_specs=[pl.BlockSpec((tm, d), lambda i: (i, 0)),
                  pl.BlockSpec((TOP_K, tm, d // 2), lambda i: (0, i + blk0, 0)),
                  pl.BlockSpec((tm, TOP_K), lambda i: (i, 0)),
                  _const_spec((1, d))],
        out_specs=pl.BlockSpec((tm, d), lambda i: (i, 0)),
        out_shape=jax.ShapeDtypeStruct((n, d), F32),
        compiler_params=_cparams(("arbitrary",), 32),
        name="final",
    )(h, yg, tw, g)


def kernel(x_prompt, x_sample, cache_k, cache_v, state_conv, g_attn_norm, w_in, lambda_q, lambda_k,
           g_subln, conv_w, w_attn_proj, w_conv_proj, w_out, g_ffn_norm, w_router, b_router, w_gu, b_gu,
           w_down, b_down, g_final):
    assert w_in.shape[0] == 1, "single-layer trunk"
    bp, tp, d = x_prompt.shape
    bs, ts, _ = x_sample.shape
    past = cache_k.shape[2]
    f = w_down.shape[2]
    n_p, n_s = bp * tp, bs * ts
    n_all = n_p + n_s

    w_in_b = w_in[0].astype(BF16)
    wcp_b = w_conv_proj[0].astype(BF16)
    wap_b = w_attn_proj[0].astype(BF16)
    wout_b = w_out[0].astype(BF16)
    wr_b = w_router[0].astype(BF16)
    bgu = b_gu[0].reshape(N_EXPERTS, 1, f, 2)
    bg, bu = bgu[..., 0], bgu[..., 1]
    bd = b_down[0].reshape(N_EXPERTS, 1, d)
    g_attn = g_attn_norm[0].reshape(1, d)
    g_ffn = g_ffn_norm[0].reshape(1, d)
    gs = g_subln[0].reshape(1, V_DIM)
    lq, lk = lambda_q[0], lambda_k[0]
    br = b_router[0].reshape(1, N_EXPERTS)

    prev_p = jnp.zeros((bp, 8, d), F32)
    prev_s = jnp.pad(state_conv[0], ((0, 0), (6, 0), (0, 0)))

    q_p, k_p, v_p, kb_p, vb_p, sga_p, gcc_p, conv_p = _inproj(x_prompt, prev_p, g_attn, w_in_b, conv_w[0], wcp_b)
    q_s, k_s, v_s, kb_s, vb_s, sga_s, gcc_s, conv_s = _inproj(x_sample, prev_s, g_attn, w_in_b, conv_w[0], wcp_b)

    on_p = _attn_prompt(lq, lk, gs, q_p, kb_p, vb_p)
    on_s = _attn_sample(lq, lk, gs, q_s, cache_k[0].reshape(bs, past, d), cache_v[0].reshape(bs, past, d),
                        kb_s, vb_s)

    flat = lambda a, n: a.reshape(n, d)
    cnt0 = jnp.zeros((1, N_EXPERTS), F32)
    h_p, hn_p, ti_p, tw_p, rk_p, cnt_p = _post(cnt0, flat(x_prompt, n_p), flat(on_p, n_p), flat(sga_p, n_p),
                                               flat(gcc_p, n_p), wap_b, wout_b, g_ffn, wr_b, br)
    h_s, hn_s, ti_s, tw_s, rk_s, cnt = _post(cnt_p, flat(x_sample, n_s), flat(on_s, n_s), flat(sga_s, n_s),
                                             flat(gcc_s, n_s), wap_b, wout_b, g_ffn, wr_b, br)

    counts = cnt[0].astype(jnp.int32)
    padded = ((counts + MOE_TILE - 1) // MOE_TILE) * MOE_TILE
    ends = jnp.cumsum(padded)
    starts = ends - padded
    n_tiles = (n_all * TOP_K) // MOE_TILE + N_EXPERTS
    tile_row = jnp.arange(n_tiles, dtype=jnp.int32) * MOE_TILE
    tile_expert = jnp.sum((tile_row[:, None] >= ends[None, :]).astype(jnp.int32), axis=1)
    tile_expert = jnp.minimum(tile_expert, N_EXPERTS - 1)
    tile_valid = (tile_row < ends[-1]).astype(jnp.int32)

    poss = [_sc_positions(starts[ti_p] + rk_p), _sc_positions(starts[ti_s] + rk_s)]
    xs = _sc_dispatch([hn_p, hn_s], poss, n_tiles * MOE_TILE)
    ys = _moe(tile_expert, tile_valid, xs, w_gu[0], bg, bu, w_down[0], bd)
    yg = _sc_gather(ys, poss, [n_p, n_s])

    y_p = _final(h_p, yg, tw_p, g_final.reshape(1, d), 0)
    y_s = _final(h_s, yg, tw_s, g_final.reshape(1, d), n_p)

    hd = N_HEADS
    return (y_p.reshape(bp, tp, d), y_s.reshape(bs, ts, d),
            k_p.reshape(1, bp, tp, hd, V_DIM), v_p.reshape(1, bp, tp, hd, V_DIM), conv_p[None],
            k_s.reshape(1, bs, ts, hd, V_DIM), v_s.reshape(1, bs, ts, hd, V_DIM), conv_s[None])
```

```python
import functools
import math

import jax
import jax.numpy as jnp
from jax import lax
from jax.experimental import pallas as pl
from jax.experimental.pallas import tpu as pltpu
from jax.experimental.pallas import tpu_sc as plsc

N_HEADS = 8
HEAD_DIM = 64
V_DIM = 2 * HEAD_DIM
CHUNK = 64
N_EXPERTS = 32
TOP_K = 4
SWIGLU_LIMIT = 7.0
SWIGLU_ALPHA = 1.702
EPS = 1e-6
NEG = -1e30
LAMBDA_INIT = 0.8 - 0.6 * math.exp(-0.3 * 0)
N_SEG = 8
Q_SCALE = HEAD_DIM ** -0.5 * math.log2(math.e)

ROW_TILE = 256
STREAM_TILE = 1024
POST_CHUNK = 1024
ATTN_TILE = 1024
MOE_TILE = 512
MIB = 1024 * 1024
SC_CORES = 2
SC_SUBCORES = 16
SC_WORKERS = SC_CORES * SC_SUBCORES
SC_MAX_CHUNK = 32

BF16 = jnp.bfloat16
F32 = jnp.float32


def _cparams(semantics, vmem_mib):
    return pltpu.CompilerParams(dimension_semantics=semantics, vmem_limit_bytes=vmem_mib * MIB)


def _const_spec(shape):
    nd = len(shape)
    return pl.BlockSpec(shape, lambda *_: (0,) * nd, pipeline_mode=pl.Buffered(1))


def _sigmoid(x):
    return 1.0 / (1.0 + jnp.exp(-x))


def _dot(a, b):
    return jnp.dot(a, b, preferred_element_type=F32)


def _dot_nt(a, b):
    return lax.dot_general(a, b, (((1,), (1,)), ((), ())), preferred_element_type=F32)


def _pack_halves(x):
    w = x.shape[1] // 2
    lo = lax.bitcast_convert_type(x[:, :w].astype(BF16).astype(F32), jnp.uint32)
    hi = lax.bitcast_convert_type(x[:, w:].astype(BF16).astype(F32), jnp.uint32)
    return lax.bitcast_convert_type((lo >> 16) | (hi & jnp.uint32(0xFFFF0000)), jnp.int32)


def _unpack_halves(words):
    u = lax.bitcast_convert_type(words, jnp.uint32)
    lo = lax.bitcast_convert_type(u << 16, F32)
    hi = lax.bitcast_convert_type(u & jnp.uint32(0xFFFF0000), F32)
    return lo, hi


def _inproj_kernel(x_ref, prev_ref, g_ref, w_ref, cw_ref, wcp_ref,
                   q_ref, k32_ref, v32_ref, kb_ref, vb_ref, sga_ref, gcc_ref, cnew_ref, ubuf):
    t = pl.program_id(1)
    tm, d = x_ref.shape[1], x_ref.shape[2]

    @pl.when(t == 0)
    def _():
        ubuf[0:8, :] = prev_ref[0]

    x = x_ref[0]
    ms = jnp.mean(x * x, axis=-1, keepdims=True)
    xn = ((x * lax.rsqrt(ms + EPS)) * g_ref[...]).astype(BF16)

    def seg(i):
        return _dot(xn, w_ref[:, i * d:(i + 1) * d])

    q_ref[0] = (seg(0) * Q_SCALE).astype(BF16)
    k = seg(1)
    k32_ref[0] = k
    kb_ref[0] = k.astype(BF16)
    v = seg(2)
    v32_ref[0] = v
    vb_ref[0] = v.astype(BF16)

    u = seg(5) * seg(3)
    ubuf[8:8 + tm, :] = u
    conv = (ubuf[6:6 + tm, :] * cw_ref[0:1, :] + ubuf[7:7 + tm, :] * cw_ref[1:2, :]
            + u * cw_ref[2:3, :])
    cpre = (seg(4) * conv).astype(BF16)
    c_out = _dot(cpre, wcp_ref[...])
    gcc_ref[0] = (_sigmoid(seg(7)) * c_out).astype(BF16)
    sga_ref[0] = _sigmoid(seg(6)).astype(BF16)
    cnew_ref[0] = ubuf[tm + 6:tm + 8, :]
    ubuf[0:8, :] = ubuf[tm:tm + 8, :]


def _inproj(x, conv_prev8, g, w_in_b, conv_w, wcp_b):
    b, t, d = x.shape
    tm = min(t, ROW_TILE)
    assert t % tm == 0 and tm % 8 == 0
    row = lambda: pl.BlockSpec((1, tm, d), lambda i, j: (i, j, 0))
    bf = jax.ShapeDtypeStruct((b, t, d), BF16)
    f32 = jax.ShapeDtypeStruct((b, t, d), F32)
    return pl.pallas_call(
        _inproj_kernel,
        grid=(b, t // tm),
        in_specs=[row(),
                  pl.BlockSpec((1, 8, d), lambda i, j: (i, 0, 0)),
                  _const_spec((1, d)),
                  _const_spec((d, N_SEG * d)),
                  _const_spec((3, d)),
                  _const_spec((d, d))],
        out_specs=[row(), row(), row(), row(), row(), row(), row(),
                   pl.BlockSpec((1, 2, d), lambda i, j: (i, 0, 0))],
        out_shape=[bf, f32, f32, bf, bf, bf, bf, jax.ShapeDtypeStruct((b, 2, d), F32)],
        scratch_shapes=[pltpu.VMEM((tm + 8, d), F32)],
        compiler_params=_cparams(("arbitrary", "arbitrary"), 56),
        name="inproj",
    )(x, conv_prev8, g, w_in_b, conv_w, wcp_b)


def _lambda(lq_ref, lk_ref):
    e = jnp.exp(jnp.sum(lq_ref[...] * lk_ref[...], axis=-1, keepdims=True))
    return e[0:1, :] - e[1:2, :] + LAMBDA_INIT


def _stack_maps(q):
    lane = lax.broadcasted_iota(jnp.int32, q.shape, 1)
    zero = jnp.zeros_like(q)
    return jnp.concatenate([jnp.where(lane < HEAD_DIM, q, zero),
                            jnp.where(lane >= HEAD_DIM, q, zero)], axis=0)


def _attn_finish(acc, l, lam, gs, tq):
    o = acc / l
    o = o[:tq] - lam * o[tq:]
    ms = jnp.mean(o * o, axis=-1, keepdims=True)
    return ((o * lax.rsqrt(ms + EPS)) * gs) * (1.0 - LAMBDA_INIT)


def _attn_prompt_kernel(lq_ref, lk_ref, gs_ref, q_ref, k_ref, v_ref, o_ref,
                        s_sc, p_sc, qq_sc, m_sc, l_sc, acc_sc, bias_sc, *, tile):
    n_q = q_ref.shape[1] // tile
    lane_tiles = [slice(c * V_DIM, (c + 1) * V_DIM) for c in range(tile // V_DIM)]
    lam = _lambda(lq_ref, lk_ref)

    def tile_rows(j):
        return pl.ds(j * tile if isinstance(j, int) else pl.multiple_of(j * tile, tile), tile)

    def rows_of(ref, j):
        return ref[0, tile_rows(j), :]

    @pl.when((pl.program_id(0) == 0) & (pl.program_id(1) == 0))
    def _():
        r = lax.broadcasted_iota(jnp.int32, bias_sc.shape, 0)
        c = lax.broadcasted_iota(jnp.int32, bias_sc.shape, 1)
        r = jnp.where(r >= tile, r - tile, r)
        bias_sc[...] = jnp.where((c // CHUNK) <= (r // CHUNK), 0.0, NEG).astype(F32)

    def key_tile(j, p):
        if isinstance(p, int):
            return j if p == 0 else p - 1
        return jnp.where(p == 0, j, p - 1)

    def diag_scores(j):
        qq = _stack_maps(rows_of(q_ref, j))
        qq_sc[...] = qq
        s_sc[...] = _dot_nt(qq, rows_of(k_ref, j)) + bias_sc[...]

    def scores(kt):
        s_sc[...] = _dot_nt(qq_sc[...], rows_of(k_ref, kt))

    def probs_v(kt):
        return _dot(p_sc[...], rows_of(v_ref, kt))

    def softmax(pv):
        m_cur = s_sc[:, lane_tiles[0]]
        for lt in lane_tiles[1:]:
            m_cur = jnp.maximum(m_cur, s_sc[:, lt])
        m_cur = jnp.max(m_cur, axis=-1, keepdims=True)
        if pv is None:
            m_new = jnp.broadcast_to(m_cur, m_sc.shape)
        else:
            m_old = m_sc[...]
            m_new = jnp.maximum(m_old, m_cur)
            alpha = jnp.exp2(m_old - m_new)
        psum = None
        for lt in lane_tiles:
            pc = jnp.exp2(s_sc[:, lt] - m_new)
            p_sc[:, lt] = pc.astype(BF16)
            psum = pc if psum is None else psum + pc
        if pv is None:
            l_sc[...] = psum
            acc_sc[...] = jnp.zeros(acc_sc.shape, F32)
        else:
            l_sc[...] = alpha * l_sc[...] + psum
            acc_sc[...] = alpha * (acc_sc[...] + pv)
        m_sc[...] = m_new

    def finish(j, last_kt):
        acc = acc_sc[...] + probs_v(last_kt)
        l = jnp.sum(l_sc[...], axis=-1, keepdims=True)
        o_ref[0, tile_rows(j), :] = _attn_finish(
            acc, l, lam, gs_ref[...], tile).astype(BF16)

    diag_scores(0)
    softmax(None)
    diag_scores(min(1, n_q - 1))

    def query_tile(j, carry):
        finish(j - 1, key_tile(j - 1, j - 1))
        softmax(None)
        scores(0)

        def body(p, c):
            pv = probs_v(key_tile(j, p - 1))
            softmax(pv)
            scores(p)
            return c

        lax.fori_loop(1, j, body, 0)
        pv = probs_v(key_tile(j, j - 1))
        softmax(pv)
        diag_scores(jnp.minimum(j + 1, n_q - 1))
        return carry

    lax.fori_loop(1, n_q, query_tile, 0)
    finish(n_q - 1, key_tile(n_q - 1, n_q - 1))


def _attn_vmem_mib(t, tile):
    rows = 2 * tile
    score_f32 = rows * tile * 4
    scratch = (2 * score_f32 + score_f32 // 2 + 3 * rows * V_DIM * 4
               + rows * V_DIM * 2)
    blocks = 2 * 4 * (t * V_DIM * 2)
    temporaries = score_f32
    return -(-(scratch + blocks + temporaries) // MIB) + 6


def _attn_prompt(lq, lk, gs, q, kb, vb):
    b, t, d = q.shape
    tile = min(t, ATTN_TILE)
    assert t % tile == 0 and tile % CHUNK == 0 and tile % V_DIM == 0
    spec = pl.BlockSpec((1, t, V_DIM), lambda bi, h: (bi, 0, h))
    rows = 2 * tile
    return pl.pallas_call(
        functools.partial(_attn_prompt_kernel, tile=tile),
        grid=(b, N_HEADS),
        in_specs=[_const_spec((2, HEAD_DIM)), _const_spec((2, HEAD_DIM)), _const_spec((1, V_DIM)),
                  spec, spec, spec],
        out_specs=spec,
        out_shape=jax.ShapeDtypeStruct((b, t, d), BF16),
        scratch_shapes=[pltpu.VMEM((rows, tile), F32), pltpu.VMEM((rows, tile), BF16),
                        pltpu.VMEM((rows, V_DIM), BF16),
                        pltpu.VMEM((rows, V_DIM), F32), pltpu.VMEM((rows, V_DIM), F32),
                        pltpu.VMEM((rows, V_DIM), F32), pltpu.VMEM((rows, tile), F32)],
        compiler_params=_cparams(("arbitrary", "arbitrary"), _attn_vmem_mib(t, tile)),
        name="attn_prompt",
    )(lq, lk, gs, q, kb, vb)


def _attn_sample_kernel(lq_ref, lk_ref, gs_ref, q_ref, ck_ref, cv_ref, kn_ref, vn_ref, o_ref):
    tq = q_ref.shape[1]
    lam = _lambda(lq_ref, lk_ref)
    for h in range(N_HEADS):
        cols = slice(h * V_DIM, (h + 1) * V_DIM)
        qq = _stack_maps(q_ref[0, :, cols])
        s_p = _dot_nt(qq, ck_ref[:, h, :].astype(BF16))
        s_n = _dot_nt(qq, kn_ref[0, :, cols])
        m = jnp.maximum(jnp.max(s_p, axis=-1, keepdims=True), jnp.max(s_n, axis=-1, keepdims=True))
        p_p = jnp.exp2(s_p - m)
        p_n = jnp.exp2(s_n - m)
        l = jnp.sum(p_p, axis=-1, keepdims=True) + jnp.sum(p_n, axis=-1, keepdims=True)
        acc = (_dot(p_p.astype(BF16), cv_ref[:, h, :].astype(BF16))
               + _dot(p_n.astype(BF16), vn_ref[0, :, cols]))
        o_ref[0, :, cols] = _attn_finish(acc, l, lam, gs_ref[...], tq).astype(BF16)


def _attn_sample(lq, lk, gs, q, cache_k, cache_v, kb, vb):
    b, t, d = q.shape
    past = cache_k.shape[2]
    qspec = pl.BlockSpec((1, t, d), lambda bi: (bi, 0, 0))
    cspec = pl.BlockSpec((None, None, past, N_HEADS, V_DIM), lambda bi: (0, bi, 0, 0, 0))
    return pl.pallas_call(
        _attn_sample_kernel,
        grid=(b,),
        in_specs=[_const_spec((2, HEAD_DIM)), _const_spec((2, HEAD_DIM)), _const_spec((1, V_DIM)),
                  qspec, cspec, cspec, qspec, qspec],
        out_specs=qspec,
        out_shape=jax.ShapeDtypeStruct((b, t, d), BF16),
        compiler_params=_cparams(("arbitrary",), 40),
        name="attn_sample",
    )(lq, lk, gs, q, cache_k, cache_v, kb, vb)


def _lane_cols(cols, dtype):
    rows = cols[0].shape[0]
    lane = lax.broadcasted_iota(jnp.int32, (rows, V_DIM), 1)
    out = jnp.zeros((rows, V_DIM), dtype)
    for kk, col in enumerate(cols):
        out = jnp.where(lane == kk, col.astype(dtype), out)
    return out


def _post_kernel(cnt_in_ref, x_ref, on_ref, sga_ref, gcc_ref, wap_ref, wout_ref, g_ref, wr_ref, br_ref,
                 h_ref, hn_ref, tw_ref, route_ref, cnt_ref, carry):
    tm = x_ref.shape[0]
    ch = min(tm, POST_CHUNK)

    @pl.when(pl.program_id(0) == 0)
    def _():
        carry[...] = cnt_in_ref[...]

    r = lax.broadcasted_iota(jnp.int32, (ch, ch), 0)
    c = lax.broadcasted_iota(jnp.int32, (ch, ch), 1)
    earlier = (c < r).astype(BF16)
    counts = carry[...]
    for c0 in range(0, tm, ch):
        rows = slice(c0, c0 + ch)
        a_out = _dot(on_ref[rows, :], wap_ref[...])
        merged = sga_ref[rows, :].astype(F32) * a_out + gcc_ref[rows, :].astype(F32)
        h = x_ref[rows, :] + _dot(merged.astype(BF16), wout_ref[...])
        h_ref[rows, :] = h
        ms = jnp.mean(h * h, axis=-1, keepdims=True)
        hn32 = (h * lax.rsqrt(ms + EPS)) * g_ref[...]
        hn_ref[rows, :] = _pack_halves(hn32)

        logits = _dot(hn32.astype(BF16), wr_ref[...]) + br_ref[...]
        lane = lax.broadcasted_iota(jnp.int32, logits.shape, 1)
        work = logits
        vals, idxs, sels = [], [], []
        for _ in range(TOP_K):
            mk = jnp.max(work, axis=-1, keepdims=True)
            ik = jnp.min(jnp.where(work == mk, lane, N_EXPERTS), axis=-1, keepdims=True)
            sel = lane == ik
            work = jnp.where(sel, -jnp.inf, work)
            vals.append(mk)
            idxs.append(ik)
            sels.append(sel)
        es = [jnp.exp(vk - vals[0]) for vk in vals]
        denom = es[0] + es[1] + es[2] + es[3]
        tw_ref[rows, :] = _lane_cols([e / denom for e in es], F32)

        picked = (sels[0] | sels[1] | sels[2] | sels[3])
        before = _dot(earlier, picked.astype(BF16)) + counts
        ranks = [jnp.sum(jnp.where(sk, before, 0.0), axis=-1, keepdims=True) for sk in sels]
        route = _lane_cols(idxs + ranks, jnp.int32)
        route_ref[:, rows] = route.T[0:2 * TOP_K, :]
        counts = counts + jnp.sum(picked.astype(F32), axis=0, keepdims=True)
    carry[...] = counts
    cnt_ref[...] = counts


def _post_vmem_mib(tm, d):
    blocks = 2 * (tm * d * (4 + 2 + 2 + 2 + 4 + 2) + tm * V_DIM * 4 + 2 * TOP_K * tm * 4)
    weights = 2 * d * d * 2 + d * N_EXPERTS * 2
    temporaries = 4 * tm * d * 4
    return -(-(blocks + weights + temporaries) // MIB) + 4


def _post(cnt_in, x, on, sga, gcc, wap_b, wout_b, g, wr_b, br):
    n, d = x.shape
    tm = min(n, STREAM_TILE)
    assert n % tm == 0 and tm % min(tm, POST_CHUNK) == 0
    row = lambda: pl.BlockSpec((tm, d), lambda i: (i, 0))
    cnt = lambda: pl.BlockSpec((1, N_EXPERTS), lambda i: (0, 0))
    return pl.pallas_call(
        _post_kernel,
        grid=(n // tm,),
        in_specs=[cnt(), row(), row(), row(), row(),
                  _const_spec((d, d)), _const_spec((d, d)), _const_spec((1, d)),
                  _const_spec((d, N_EXPERTS)), _const_spec((1, N_EXPERTS))],
        out_specs=[row(), pl.BlockSpec((tm, d // 2), lambda i: (i, 0)),
                   pl.BlockSpec((tm, V_DIM), lambda i: (i, 0)),
                   pl.BlockSpec((2 * TOP_K, tm), lambda i: (0, i)), cnt()],
        out_shape=[jax.ShapeDtypeStruct((n, d), F32), jax.ShapeDtypeStruct((n, d // 2), jnp.int32),
                   jax.ShapeDtypeStruct((n, V_DIM), F32),
                   jax.ShapeDtypeStruct((2 * TOP_K, n), jnp.int32),
                   jax.ShapeDtypeStruct((1, N_EXPERTS), F32)],
        scratch_shapes=[pltpu.VMEM((1, N_EXPERTS), F32)],
        compiler_params=_cparams(("arbitrary",), _post_vmem_mib(tm, d)),
        name="post",
    )(cnt_in, x, on, sga, gcc, wap_b, wout_b, g, wr_b, br)


def _moe_kernel(te_ref, tv_ref, x_ref, wgu_ref, bg_ref, bu_ref, wd_ref, bd_ref, y_ref, wgu_sc, wd_sc):
    t = pl.program_id(0)
    valid = tv_ref[t] > 0
    new_expert = (t == 0) | (te_ref[t] != te_ref[jnp.maximum(t - 1, 0)])
    f = wd_ref.shape[1]
    pair = 2 * V_DIM

    @pl.when(valid & new_expert)
    def _():
        r = lax.broadcasted_iota(jnp.int32, (pair, pair), 0)
        c = lax.broadcasted_iota(jnp.int32, (pair, pair), 1)
        src = jnp.where(c < V_DIM, 2 * c, 2 * (c - V_DIM) + 1)
        perm = (r == src).astype(BF16)
        for b in range(2 * f // pair):
            blk = wgu_ref[0, :, b * pair:(b + 1) * pair].astype(BF16)
            wgu_sc[:, b * pair:(b + 1) * pair] = _dot(blk, perm).astype(BF16)
        wd_sc[...] = wd_ref[0].astype(BF16)

    @pl.when(valid)
    def _():
        x_lo, x_hi = _unpack_halves(x_ref[...])
        x = jnp.concatenate([x_lo.astype(BF16), x_hi.astype(BF16)], axis=1)
        hgu = _dot(x, wgu_sc[...])
        acts = []
        for b in range(f // V_DIM):
            cols = slice(b * V_DIM, (b + 1) * V_DIM)
            gate = jnp.minimum(hgu[:, b * pair:b * pair + V_DIM] + bg_ref[0, :, cols], SWIGLU_LIMIT)
            up = jnp.clip(hgu[:, b * pair + V_DIM:(b + 1) * pair] + bu_ref[0, :, cols],
                          -SWIGLU_LIMIT, SWIGLU_LIMIT)
            glu = gate * _sigmoid(SWIGLU_ALPHA * gate)
            acts.append((glu * (up + 1.0)).astype(BF16))
        y = _dot(jnp.concatenate(acts, axis=1), wd_sc[...]) + bd_ref[0]
        y_ref[...] = _pack_halves(y)


def _moe(tile_expert, tile_valid, xs, w_gu, bg, bu, w_down, bd):
    p = xs.shape[0]
    f, d = w_down.shape[1:]
    n_tiles = p // MOE_TILE
    row = pl.BlockSpec((MOE_TILE, d // 2), lambda i, te, tv: (i, 0))
    wspec = lambda a, b_: pl.BlockSpec((1, a, b_), lambda i, te, tv: (te[i], 0, 0))
    return pl.pallas_call(
        _moe_kernel,
        grid_spec=pltpu.PrefetchScalarGridSpec(
            num_scalar_prefetch=2,
            grid=(n_tiles,),
            in_specs=[row, wspec(d, 2 * f), wspec(1, f), wspec(1, f), wspec(f, d), wspec(1, d)],
            out_specs=row,
            scratch_shapes=[pltpu.VMEM((d, 2 * f), BF16), pltpu.VMEM((f, d), BF16)]),
        out_shape=jax.ShapeDtypeStruct((p, d // 2), jnp.int32),
        compiler_params=_cparams(("arbitrary",), 48),
        name="moe",
    )(tile_expert, tile_valid, xs, w_gu, bg, bu, w_down, bd)


def _sc_split(n):
    per_w = n // SC_WORKERS
    assert per_w * SC_WORKERS == n and per_w % 8 == 0, n
    ch = max(c for c in range(8, SC_MAX_CHUNK + 1, 8) if per_w % c == 0)
    return per_w, ch


def _sc_positions(pos):
    per_w, ch = _sc_split(pos.shape[1])
    return pos.reshape(TOP_K, SC_WORKERS, per_w // ch, ch).transpose(1, 2, 0, 3)


def _sc_mesh():
    return plsc.VectorSubcoreMesh(core_axis_name="c", subcore_axis_name="s",
                                  num_cores=SC_CORES, num_subcores=SC_SUBCORES)


def _sc_worker():
    return lax.axis_index("s") * SC_CORES + lax.axis_index("c")


def _sc_dispatch(srcs, poss, p_rows):
    w = srcs[0].shape[1]
    geo = [_sc_split(s.shape[0]) for s in srcs]
    nseg = len(srcs)
    scratch = []
    for _, ch in geo:
        scratch += [pltpu.VMEM((TOP_K, ch), jnp.int32), pltpu.VMEM((ch, w), jnp.int32)]
    scratch.append(pltpu.SemaphoreType.DMA)

    def body(*refs):
        src_refs, pos_refs, xs_ref = refs[:nseg], refs[nseg:2 * nseg], refs[2 * nseg]
        scr = refs[2 * nseg + 1:]
        sem = scr[-1]
        wid = _sc_worker()
        for g in range(nseg):
            per_w, ch = geo[g]

            @pl.loop(0, per_w // ch)
            def _(c, src=src_refs[g], pos=pos_refs[g], idx_v=scr[2 * g], rows_v=scr[2 * g + 1],
                  per_w=per_w, ch=ch):
                base = pl.multiple_of(wid * per_w + c * ch, 8)
                pltpu.sync_copy(pos.at[wid, c], idx_v)
                pltpu.sync_copy(src.at[pl.ds(base, ch)], rows_v)
                copies = [pltpu.async_copy(rows_v, xs_ref.at[idx_v.at[kk]], sem) for kk in range(TOP_K)]
                for cp in copies:
                    cp.wait()

    return pl.kernel(body, out_type=jax.ShapeDtypeStruct((p_rows, w), jnp.int32), mesh=_sc_mesh(),
                     scratch_types=scratch, name="sc_dispatch")(*srcs, *poss)


def _sc_gather(ys, poss, seg_rows):
    w = ys.shape[1]
    geo = [_sc_split(n) for n in seg_rows]
    nseg = len(seg_rows)
    row0 = [sum(seg_rows[:g]) for g in range(nseg)]
    scratch = []
    for _, ch in geo:
        scratch += [pltpu.VMEM((TOP_K, ch), jnp.int32), pltpu.VMEM((TOP_K, ch, w), jnp.int32)]
    scratch += [pltpu.SemaphoreType.DMA, pltpu.SemaphoreType.DMA]

    def body(*refs):
        ys_ref, pos_refs, out_ref = refs[0], refs[1:1 + nseg], refs[1 + nseg]
        scr = refs[2 + nseg:]
        sem_g, sem_w = scr[-2], scr[-1]
        wid = _sc_worker()
        for g in range(nseg):
            per_w, ch = geo[g]

            @pl.loop(0, per_w // ch)
            def _(c, pos=pos_refs[g], idx_v=scr[2 * g], rows_v=scr[2 * g + 1], per_w=per_w, ch=ch,
                  r0=row0[g]):
                base = pl.multiple_of(r0 + wid * per_w + c * ch, 8)
                pltpu.sync_copy(pos.at[wid, c], idx_v)
                gathers = [pltpu.async_copy(ys_ref.at[idx_v.at[kk]], rows_v.at[kk], sem_g)
                           for kk in range(TOP_K)]
                writes = []
                for kk in range(TOP_K):
                    gathers[kk].wait()
                    writes.append(pltpu.async_copy(rows_v.at[kk], out_ref.at[kk, pl.ds(base, ch)], sem_w))
                for cp in writes:
                    cp.wait()

    return pl.kernel(body, out_type=jax.ShapeDtypeStruct((TOP_K, sum(seg_rows), w), jnp.int32),
                     mesh=_sc_mesh(), scratch_types=scratch, name="sc_gather")(ys, *poss)


def _final_kernel(h_ref, yg_ref, tw_ref, g_ref, o_ref):
    tw = tw_ref[...]
    acc = None
    for kk in range(TOP_K):
        lo, hi = _unpack_halves(yg_ref[kk])
        part = jnp.concatenate([lo, hi], axis=1) * tw[:, kk:kk + 1]
        acc = part if acc is None else acc + part
    h = h_ref[...] + acc
    ms = jnp.mean(h * h, axis=-1, keepdims=True)
    o_ref[...] = (h * lax.rsqrt(ms + EPS)) * g_ref[...]


def _final(h, yg, tw, g, row0):
    n, d = h.shape
    tm = min(n, STREAM_TILE)
    assert n % tm == 0 and row0 % tm == 0
    blk0 = row0 // tm
    return pl.pallas_call(
        _final_kernel,
        grid=(n // tm,),
        in_specs=[pl.BlockSpec((tm, d), lambda i: (i, 0)),
                  pl.BlockSpec((TOP_K, tm, d // 2), lambda i: (0, i + blk0, 0)),
                  pl.BlockSpec((tm, V_DIM), lambda i: (i, 0)),
                  _const_spec((1, d))],
        out_specs=pl.BlockSpec((tm, d), lambda i: (i, 0)),
        out_shape=jax.ShapeDtypeStruct((n, d), F32),
        compiler_params=_cparams(("arbitrary",), -(-2 * tm * (2 * d * 4 + TOP_K * d * 2 + V_DIM * 4) // MIB) + 6),
        name="final",
    )(h, yg, tw, g)


def kernel(x_prompt, x_sample, cache_k, cache_v, state_conv, g_attn_norm, w_in, lambda_q, lambda_k,
           g_subln, conv_w, w_attn_proj, w_conv_proj, w_out, g_ffn_norm, w_router, b_router, w_gu, b_gu,
           w_down, b_down, g_final):
    assert w_in.shape[0] == 1, "single-layer trunk"
    bp, tp, d = x_prompt.shape
    bs, ts, _ = x_sample.shape
    past = cache_k.shape[2]
    f = w_down.shape[2]
    n_p, n_s = bp * tp, bs * ts
    n_all = n_p + n_s

    w_in_b = w_in[0].astype(BF16)
    wcp_b = w_conv_proj[0].astype(BF16)
    wap_b = w_attn_proj[0].astype(BF16)
    wout_b = w_out[0].astype(BF16)
    wr_b = w_router[0].astype(BF16)
    bgu = b_gu[0].reshape(N_EXPERTS, 1, f, 2)
    bg, bu = bgu[..., 0], bgu[..., 1]
    bd = b_down[0].reshape(N_EXPERTS, 1, d)
    g_attn = g_attn_norm[0].reshape(1, d)
    g_ffn = g_ffn_norm[0].reshape(1, d)
    gs = g_subln[0].reshape(1, V_DIM)
    lq, lk = lambda_q[0], lambda_k[0]
    br = b_router[0].reshape(1, N_EXPERTS)

    prev_p = jnp.zeros((bp, 8, d), F32)
    prev_s = jnp.pad(state_conv[0], ((0, 0), (6, 0), (0, 0)))

    q_p, k_p, v_p, kb_p, vb_p, sga_p, gcc_p, conv_p = _inproj(x_prompt, prev_p, g_attn, w_in_b, conv_w[0], wcp_b)
    q_s, k_s, v_s, kb_s, vb_s, sga_s, gcc_s, conv_s = _inproj(x_sample, prev_s, g_attn, w_in_b, conv_w[0], wcp_b)

    on_p = _attn_prompt(lq, lk, gs, q_p, kb_p, vb_p)
    on_s = _attn_sample(lq, lk, gs, q_s, cache_k, cache_v, kb_s, vb_s)

    flat = lambda a, n: a.reshape(n, d)
    cnt0 = jnp.zeros((1, N_EXPERTS), F32)
    h_p, hn_p, tw_p, route_p, cnt_p = _post(cnt0, flat(x_prompt, n_p), flat(on_p, n_p), flat(sga_p, n_p),
                                            flat(gcc_p, n_p), wap_b, wout_b, g_ffn, wr_b, br)
    h_s, hn_s, tw_s, route_s, cnt = _post(cnt_p, flat(x_sample, n_s), flat(on_s, n_s), flat(sga_s, n_s),
                                          flat(gcc_s, n_s), wap_b, wout_b, g_ffn, wr_b, br)

    counts = cnt[0].astype(jnp.int32)
    padded = ((counts + MOE_TILE - 1) // MOE_TILE) * MOE_TILE
    ends = jnp.cumsum(padded)
    starts = ends - padded
    n_tiles = (n_all * TOP_K) // MOE_TILE + N_EXPERTS
    tile_row = jnp.arange(n_tiles, dtype=jnp.int32) * MOE_TILE
    tile_expert = jnp.sum((tile_row[:, None] >= ends[None, :]).astype(jnp.int32), axis=1)
    tile_expert = jnp.minimum(tile_expert, N_EXPERTS - 1)
    tile_valid = (tile_row < ends[-1]).astype(jnp.int32)

    poss = [_sc_positions(starts[route[:TOP_K]] + route[TOP_K:]) for route in (route_p, route_s)]
    xs = _sc_dispatch([hn_p, hn_s], poss, n_tiles * MOE_TILE)
    ys = _moe(tile_expert, tile_valid, xs, w_gu[0], bg, bu, w_down[0], bd)
    yg = _sc_gather(ys, poss, [n_p, n_s])

    y_p = _final(h_p, yg, tw_p, g_final.reshape(1, d), 0)
    y_s = _final(h_s, yg, tw_s, g_final.reshape(1, d), n_p)

    hd = N_HEADS
    return (y_p.reshape(bp, tp, d), y_s.reshape(bs, ts, d),
            k_p.reshape(1, bp, tp, hd, V_DIM), v_p.reshape(1, bp, tp, hd, V_DIM), conv_p[None],
            k_s.reshape(1, bs, ts, hd, V_DIM), v_s.reshape(1, bs, ts, hd, V_DIM), conv_s[None])
```

```python
import functools
import math

import jax
import jax.numpy as jnp
from jax import lax
from jax.experimental import pallas as pl
from jax.experimental.pallas import tpu as pltpu
from jax.experimental.pallas import tpu_sc as plsc

N_HEADS = 8
HEAD_DIM = 64
V_DIM = 2 * HEAD_DIM
CHUNK = 64
N_EXPERTS = 32
TOP_K = 4
SWIGLU_LIMIT = 7.0
SWIGLU_ALPHA = 1.702
EPS = 1e-6
NEG = -1e30
LAMBDA_INIT = 0.8 - 0.6 * math.exp(-0.3 * 0)
N_SEG = 8
Q_SCALE = HEAD_DIM ** -0.5 * math.log2(math.e)

ROW_TILE = 256
STREAM_TILE = 1024
POST_CHUNK = 1024
ATTN_TILE = 1024
MOE_TILE = 512
MIB = 1024 * 1024
SC_CORES = 2
SC_SUBCORES = 16
SC_WORKERS = SC_CORES * SC_SUBCORES
SC_MAX_CHUNK = 32

BF16 = jnp.bfloat16
F32 = jnp.float32


def _cparams(semantics, vmem_mib):
    return pltpu.CompilerParams(dimension_semantics=semantics, vmem_limit_bytes=vmem_mib * MIB)


def _const_spec(shape):
    nd = len(shape)
    return pl.BlockSpec(shape, lambda *_: (0,) * nd, pipeline_mode=pl.Buffered(1))


def _sigmoid(x):
    return 1.0 / (1.0 + jnp.exp(-x))


def _dot(a, b):
    return jnp.dot(a, b, preferred_element_type=F32)


def _dot_nt(a, b):
    return lax.dot_general(a, b, (((1,), (1,)), ((), ())), preferred_element_type=F32)


def _pack_halves(x):
    w = x.shape[1] // 2
    lo = lax.bitcast_convert_type(x[:, :w].astype(BF16).astype(F32), jnp.uint32)
    hi = lax.bitcast_convert_type(x[:, w:].astype(BF16).astype(F32), jnp.uint32)
    return lax.bitcast_convert_type((lo >> 16) | (hi & jnp.uint32(0xFFFF0000)), jnp.int32)


def _unpack_halves(words):
    u = lax.bitcast_convert_type(words, jnp.uint32)
    lo = lax.bitcast_convert_type(u << 16, F32)
    hi = lax.bitcast_convert_type(u & jnp.uint32(0xFFFF0000), F32)
    return lo, hi


def _inproj_kernel(x_ref, prev_ref, g_ref, w_ref, cw_ref, wcp_ref,
                   q_ref, k32_ref, v32_ref, kb_ref, vb_ref, sga_ref, gcc_ref, cnew_ref, ubuf):
    t = pl.program_id(1)
    tm, d = x_ref.shape[1], x_ref.shape[2]

    @pl.when(t == 0)
    def _():
        ubuf[0:8, :] = prev_ref[0]

    x = x_ref[0]
    ms = jnp.mean(x * x, axis=-1, keepdims=True)
    xn = ((x * lax.rsqrt(ms + EPS)) * g_ref[...]).astype(BF16)

    def seg(i):
        return _dot(xn, w_ref[:, i * d:(i + 1) * d])

    q_ref[0] = (seg(0) * Q_SCALE).astype(BF16)
    k = seg(1)
    k32_ref[0] = k
    kb_ref[0] = k.astype(BF16)
    v = seg(2)
    v32_ref[0] = v
    vb_ref[0] = v.astype(BF16)

    u = seg(5) * seg(3)
    ubuf[8:8 + tm, :] = u
    conv = (ubuf[6:6 + tm, :] * cw_ref[0:1, :] + ubuf[7:7 + tm, :] * cw_ref[1:2, :]
            + u * cw_ref[2:3, :])
    cpre = (seg(4) * conv).astype(BF16)
    c_out = _dot(cpre, wcp_ref[...])
    gcc_ref[0] = (_sigmoid(seg(7)) * c_out).astype(BF16)
    sga_ref[0] = _sigmoid(seg(6)).astype(BF16)
    cnew_ref[0] = ubuf[tm + 6:tm + 8, :]
    ubuf[0:8, :] = ubuf[tm:tm + 8, :]


def _inproj(x, conv_prev8, g, w_in_b, conv_w, wcp_b):
    b, t, d = x.shape
    tm = min(t, ROW_TILE)
    assert t % tm == 0 and tm % 8 == 0
    row = lambda: pl.BlockSpec((1, tm, d), lambda i, j: (i, j, 0))
    bf = jax.ShapeDtypeStruct((b, t, d), BF16)
    f32 = jax.ShapeDtypeStruct((b, t, d), F32)
    return pl.pallas_call(
        _inproj_kernel,
        grid=(b, t // tm),
        in_specs=[row(),
                  pl.BlockSpec((1, 8, d), lambda i, j: (i, 0, 0)),
                  _const_spec((1, d)),
                  _const_spec((d, N_SEG * d)),
                  _const_spec((3, d)),
                  _const_spec((d, d))],
        out_specs=[row(), row(), row(), row(), row(), row(), row(),
                   pl.BlockSpec((1, 2, d), lambda i, j: (i, 0, 0))],
        out_shape=[bf, f32, f32, bf, bf, bf, bf, jax.ShapeDtypeStruct((b, 2, d), F32)],
        scratch_shapes=[pltpu.VMEM((tm + 8, d), F32)],
        compiler_params=_cparams(("arbitrary", "arbitrary"), 56),
        name="inproj",
    )(x, conv_prev8, g, w_in_b, conv_w, wcp_b)


def _lambda(lq_ref, lk_ref):
    e = jnp.exp(jnp.sum(lq_ref[...] * lk_ref[...], axis=-1, keepdims=True))
    return e[0:1, :] - e[1:2, :] + LAMBDA_INIT


def _stack_maps(q):
    lane = lax.broadcasted_iota(jnp.int32, q.shape, 1)
    zero = jnp.zeros_like(q)
    return jnp.concatenate([jnp.where(lane < HEAD_DIM, q, zero),
                            jnp.where(lane >= HEAD_DIM, q, zero)], axis=0)


def _attn_finish(acc, l, lam, gs, tq):
    o = acc / l
    o = o[:tq] - lam * o[tq:]
    ms = jnp.mean(o * o, axis=-1, keepdims=True)
    return ((o * lax.rsqrt(ms + EPS)) * gs) * (1.0 - LAMBDA_INIT)


def _attn_prompt_kernel(lq_ref, lk_ref, gs_ref, q_ref, k_ref, v_ref, o_ref,
                        s_sc, p_sc, qq_sc, m_sc, l_sc, acc_sc, bias_sc, *, tile):
    n_q = q_ref.shape[1] // tile
    lane_tiles = [slice(c * V_DIM, (c + 1) * V_DIM) for c in range(tile // V_DIM)]
    lam = _lambda(lq_ref, lk_ref)

    def tile_rows(j):
        return pl.ds(j * tile if isinstance(j, int) else pl.multiple_of(j * tile, tile), tile)

    def rows_of(ref, j):
        return ref[0, tile_rows(j), :]

    @pl.when((pl.program_id(0) == 0) & (pl.program_id(1) == 0))
    def _():
        r = lax.broadcasted_iota(jnp.int32, bias_sc.shape, 0)
        c = lax.broadcasted_iota(jnp.int32, bias_sc.shape, 1)
        r = jnp.where(r >= tile, r - tile, r)
        bias_sc[...] = jnp.where((c // CHUNK) <= (r // CHUNK), 0.0, NEG).astype(F32)

    def key_tile(j, p):
        if isinstance(p, int):
            return j if p == 0 else p - 1
        return jnp.where(p == 0, j, p - 1)

    def diag_scores(j):
        qq = _stack_maps(rows_of(q_ref, j))
        qq_sc[...] = qq
        s_sc[...] = _dot_nt(qq, rows_of(k_ref, j)) + bias_sc[...]

    def scores(kt):
        s_sc[...] = _dot_nt(qq_sc[...], rows_of(k_ref, kt))

    def probs_v(kt):
        return _dot(p_sc[...], rows_of(v_ref, kt))

    def softmax(pv):
        m_cur = s_sc[:, lane_tiles[0]]
        for lt in lane_tiles[1:]:
            m_cur = jnp.maximum(m_cur, s_sc[:, lt])
        m_cur = jnp.max(m_cur, axis=-1, keepdims=True)
        if pv is None:
            m_new = jnp.broadcast_to(m_cur, m_sc.shape)
        else:
            m_old = m_sc[...]
            m_new = jnp.maximum(m_old, m_cur)
            alpha = jnp.exp2(m_old - m_new)
        psum = None
        for lt in lane_tiles:
            pc = jnp.exp2(s_sc[:, lt] - m_new)
            p_sc[:, lt] = pc.astype(BF16)
            psum = pc if psum is None else psum + pc
        if pv is None:
            l_sc[...] = psum
            acc_sc[...] = jnp.zeros(acc_sc.shape, F32)
        else:
            l_sc[...] = alpha * l_sc[...] + psum
            acc_sc[...] = alpha * (acc_sc[...] + pv)
        m_sc[...] = m_new

    def finish(j, last_kt):
        acc = acc_sc[...] + probs_v(last_kt)
        l = jnp.sum(l_sc[...], axis=-1, keepdims=True)
        o_ref[0, tile_rows(j), :] = _attn_finish(
            acc, l, lam, gs_ref[...], tile).astype(BF16)

    diag_scores(0)
    softmax(None)
    diag_scores(min(1, n_q - 1))

    def query_tile(j, carry):
        finish(j - 1, key_tile(j - 1, j - 1))
        softmax(None)
        scores(0)

        def body(p, c):
            pv = probs_v(key_tile(j, p - 1))
            softmax(pv)
            scores(p)
            return c

        lax.fori_loop(1, j, body, 0)
        pv = probs_v(key_tile(j, j - 1))
        softmax(pv)
        diag_scores(jnp.minimum(j + 1, n_q - 1))
        return carry

    lax.fori_loop(1, n_q, query_tile, 0)
    finish(n_q - 1, key_tile(n_q - 1, n_q - 1))


def _attn_vmem_mib(t, tile):
    rows = 2 * tile
    score_f32 = rows * tile * 4
    scratch = (2 * score_f32 + score_f32 // 2 + 3 * rows * V_DIM * 4
               + rows * V_DIM * 2)
    blocks = 2 * 4 * (t * V_DIM * 2)
    temporaries = score_f32
    return -(-(scratch + blocks + temporaries) // MIB) + 6


def _attn_prompt(lq, lk, gs, q, kb, vb):
    b, t, d = q.shape
    tile = min(t, ATTN_TILE)
    assert t % tile == 0 and tile % CHUNK == 0 and tile % V_DIM == 0
    spec = pl.BlockSpec((1, t, V_DIM), lambda bi, h: (bi, 0, h))
    rows = 2 * tile
    return pl.pallas_call(
        functools.partial(_attn_prompt_kernel, tile=tile),
        grid=(b, N_HEADS),
        in_specs=[_const_spec((2, HEAD_DIM)), _const_spec((2, HEAD_DIM)), _const_spec((1, V_DIM)),
                  spec, spec, spec],
        out_specs=spec,
        out_shape=jax.ShapeDtypeStruct((b, t, d), BF16),
        scratch_shapes=[pltpu.VMEM((rows, tile), F32), pltpu.VMEM((rows, tile), BF16),
                        pltpu.VMEM((rows, V_DIM), BF16),
                        pltpu.VMEM((rows, V_DIM), F32), pltpu.VMEM((rows, V_DIM), F32),
                        pltpu.VMEM((rows, V_DIM), F32), pltpu.VMEM((rows, tile), F32)],
        compiler_params=_cparams(("arbitrary", "arbitrary"), _attn_vmem_mib(t, tile)),
        name="attn_prompt",
    )(lq, lk, gs, q, kb, vb)


def _attn_sample_kernel(lq_ref, lk_ref, gs_ref, q_ref, ck_ref, cv_ref, kn_ref, vn_ref, o_ref):
    tq = q_ref.shape[1]
    lam = _lambda(lq_ref, lk_ref)
    for h in range(N_HEADS):
        cols = slice(h * V_DIM, (h + 1) * V_DIM)
        qq = _stack_maps(q_ref[0, :, cols])
        s_p = _dot_nt(qq, ck_ref[:, h, :].astype(BF16))
        s_n = _dot_nt(qq, kn_ref[0, :, cols])
        m = jnp.maximum(jnp.max(s_p, axis=-1, keepdims=True), jnp.max(s_n, axis=-1, keepdims=True))
        p_p = jnp.exp2(s_p - m)
        p_n = jnp.exp2(s_n - m)
        l = jnp.sum(p_p, axis=-1, keepdims=True) + jnp.sum(p_n, axis=-1, keepdims=True)
        acc = (_dot(p_p.astype(BF16), cv_ref[:, h, :].astype(BF16))
               + _dot(p_n.astype(BF16), vn_ref[0, :, cols]))
        o_ref[0, :, cols] = _attn_finish(acc, l, lam, gs_ref[...], tq).astype(BF16)


def _attn_sample(lq, lk, gs, q, cache_k, cache_v, kb, vb):
    b, t, d = q.shape
    past = cache_k.shape[2]
    qspec = pl.BlockSpec((1, t, d), lambda bi: (bi, 0, 0))
    cspec = pl.BlockSpec((None, None, past, N_HEADS, V_DIM), lambda bi: (0, bi, 0, 0, 0))
    return pl.pallas_call(
        _attn_sample_kernel,
        grid=(b,),
        in_specs=[_const_spec((2, HEAD_DIM)), _const_spec((2, HEAD_DIM)), _const_spec((1, V_DIM)),
                  qspec, cspec, cspec, qspec, qspec],
        out_specs=qspec,
        out_shape=jax.ShapeDtypeStruct((b, t, d), BF16),
        compiler_params=_cparams(("arbitrary",), 40),
        name="attn_sample",
    )(lq, lk, gs, q, cache_k, cache_v, kb, vb)


def _lane_cols(cols, dtype):
    rows = cols[0].shape[0]
    lane = lax.broadcasted_iota(jnp.int32, (rows, V_DIM), 1)
    out = jnp.zeros((rows, V_DIM), dtype)
    for kk, col in enumerate(cols):
        out = jnp.where(lane == kk, col.astype(dtype), out)
    return out


def _post_kernel(cnt_in_ref, x_ref, on_ref, sga_ref, gcc_ref, wap_ref, wout_ref, g_ref, wr_ref, br_ref,
                 k_out_ref, v_out_ref, h_ref, hn_ref, tw_ref, route_ref, cnt_ref, carry):
    tm = x_ref.shape[0]
    ch = min(tm, POST_CHUNK)

    @pl.when(pl.program_id(0) == 0)
    def _():
        carry[...] = cnt_in_ref[...]

    r = lax.broadcasted_iota(jnp.int32, (ch, ch), 0)
    c = lax.broadcasted_iota(jnp.int32, (ch, ch), 1)
    earlier = (c < r).astype(BF16)
    counts = carry[...]
    for c0 in range(0, tm, ch):
        rows = slice(c0, c0 + ch)
        a_out = _dot(on_ref[rows, :], wap_ref[...])
        merged = sga_ref[rows, :].astype(F32) * a_out + gcc_ref[rows, :].astype(F32)
        h = x_ref[rows, :] + _dot(merged.astype(BF16), wout_ref[...])
        h_ref[rows, :] = h
        ms = jnp.mean(h * h, axis=-1, keepdims=True)
        hn32 = (h * lax.rsqrt(ms + EPS)) * g_ref[...]
        hn_ref[rows, :] = _pack_halves(hn32)

        logits = _dot(hn32.astype(BF16), wr_ref[...]) + br_ref[...]
        lane = lax.broadcasted_iota(jnp.int32, logits.shape, 1)
        work = logits
        vals, idxs, sels = [], [], []
        for _ in range(TOP_K):
            mk = jnp.max(work, axis=-1, keepdims=True)
            ik = jnp.min(jnp.where(work == mk, lane, N_EXPERTS), axis=-1, keepdims=True)
            sel = lane == ik
            work = jnp.where(sel, -jnp.inf, work)
            vals.append(mk)
            idxs.append(ik)
            sels.append(sel)
        es = [jnp.exp(vk - vals[0]) for vk in vals]
        denom = es[0] + es[1] + es[2] + es[3]
        tw_ref[rows, :] = _lane_cols([e / denom for e in es], F32)

        picked = (sels[0] | sels[1] | sels[2] | sels[3])
        before = _dot(earlier, picked.astype(BF16)) + counts
        ranks = [jnp.sum(jnp.where(sk, before, 0.0), axis=-1, keepdims=True) for sk in sels]
        route = _lane_cols(idxs + ranks, jnp.int32)
        route_ref[:, rows] = route.T[0:2 * TOP_K, :]
        counts = counts + jnp.sum(picked.astype(F32), axis=0, keepdims=True)
    carry[...] = counts
    cnt_ref[...] = counts


def _post_vmem_mib(tm, d):
    blocks = 2 * (tm * d * (4 + 2 + 2 + 2 + 4 + 2) + tm * V_DIM * 4 + 2 * TOP_K * tm * 4)
    weights = 2 * d * d * 2 + d * N_EXPERTS * 2
    temporaries = 4 * tm * d * 4
    return -(-(blocks + weights + temporaries) // MIB) + 4


def _post(cnt_in, x, on, sga, gcc, wap_b, wout_b, g, wr_b, br, k_out, v_out):
    n, d = x.shape
    tm = min(n, STREAM_TILE)
    assert n % tm == 0 and tm % min(tm, POST_CHUNK) == 0
    row = lambda: pl.BlockSpec((tm, d), lambda i: (i, 0))
    cnt = lambda: pl.BlockSpec((1, N_EXPERTS), lambda i: (0, 0))
    return pl.pallas_call(
        _post_kernel,
        grid=(n // tm,),
        in_specs=[cnt(), row(), row(), row(), row(),
                  _const_spec((d, d)), _const_spec((d, d)), _const_spec((1, d)),
                  _const_spec((d, N_EXPERTS)), _const_spec((1, N_EXPERTS)),
                  pl.BlockSpec(memory_space=pl.ANY), pl.BlockSpec(memory_space=pl.ANY)],
        out_specs=[row(), pl.BlockSpec((tm, d // 2), lambda i: (i, 0)),
                   pl.BlockSpec((tm, V_DIM), lambda i: (i, 0)),
                   pl.BlockSpec((2 * TOP_K, tm), lambda i: (0, i)), cnt()],
        out_shape=[jax.ShapeDtypeStruct((n, d), F32), jax.ShapeDtypeStruct((n, d // 2), jnp.int32),
                   jax.ShapeDtypeStruct((n, V_DIM), F32),
                   jax.ShapeDtypeStruct((2 * TOP_K, n), jnp.int32),
                   jax.ShapeDtypeStruct((1, N_EXPERTS), F32)],
        scratch_shapes=[pltpu.VMEM((1, N_EXPERTS), F32)],
        compiler_params=_cparams(("arbitrary",), _post_vmem_mib(tm, d)),
        name="post",
    )(cnt_in, x, on, sga, gcc, wap_b, wout_b, g, wr_b, br, k_out, v_out)


def _moe_kernel(te_ref, tv_ref, x_ref, wgu_ref, bg_ref, bu_ref, wd_ref, bd_ref, y_ref, wgu_sc, wd_sc):
    t = pl.program_id(0)
    valid = tv_ref[t] > 0
    new_expert = (t == 0) | (te_ref[t] != te_ref[jnp.maximum(t - 1, 0)])
    f = wd_ref.shape[1]
    pair = 2 * V_DIM

    @pl.when(valid & new_expert)
    def _():
        r = lax.broadcasted_iota(jnp.int32, (pair, pair), 0)
        c = lax.broadcasted_iota(jnp.int32, (pair, pair), 1)
        src = jnp.where(c < V_DIM, 2 * c, 2 * (c - V_DIM) + 1)
        perm = (r == src).astype(BF16)
        for b in range(2 * f // pair):
            blk = wgu_ref[0, :, b * pair:(b + 1) * pair].astype(BF16)
            wgu_sc[:, b * pair:(b + 1) * pair] = _dot(blk, perm).astype(BF16)
        wd_sc[...] = wd_ref[0].astype(BF16)

    @pl.when(valid)
    def _():
        x_lo, x_hi = _unpack_halves(x_ref[...])
        x = jnp.concatenate([x_lo.astype(BF16), x_hi.astype(BF16)], axis=1)
        hgu = _dot(x, wgu_sc[...])
        acts = []
        for b in range(f // V_DIM):
            cols = slice(b * V_DIM, (b + 1) * V_DIM)
            gate = jnp.minimum(hgu[:, b * pair:b * pair + V_DIM] + bg_ref[0, :, cols], SWIGLU_LIMIT)
            up = jnp.clip(hgu[:, b * pair + V_DIM:(b + 1) * pair] + bu_ref[0, :, cols],
                          -SWIGLU_LIMIT, SWIGLU_LIMIT)
            glu = gate * _sigmoid(SWIGLU_ALPHA * gate)
            acts.append((glu * (up + 1.0)).astype(BF16))
        y = _dot(jnp.concatenate(acts, axis=1), wd_sc[...]) + bd_ref[0]
        y_ref[...] = _pack_halves(y)


def _moe(tile_expert, tile_valid, xs, w_gu, bg, bu, w_down, bd):
    p = xs.shape[0]
    f, d = w_down.shape[1:]
    n_tiles = p // MOE_TILE
    row = pl.BlockSpec((MOE_TILE, d // 2), lambda i, te, tv: (i, 0))
    wspec = lambda a, b_: pl.BlockSpec((1, a, b_), lambda i, te, tv: (te[i], 0, 0))
    return pl.pallas_call(
        _moe_kernel,
        grid_spec=pltpu.PrefetchScalarGridSpec(
            num_scalar_prefetch=2,
            grid=(n_tiles,),
            in_specs=[row, wspec(d, 2 * f), wspec(1, f), wspec(1, f), wspec(f, d), wspec(1, d)],
            out_specs=row,
            scratch_shapes=[pltpu.VMEM((d, 2 * f), BF16), pltpu.VMEM((f, d), BF16)]),
        out_shape=jax.ShapeDtypeStruct((p, d // 2), jnp.int32),
        compiler_params=_cparams(("arbitrary",), 48),
        name="moe",
    )(tile_expert, tile_valid, xs, w_gu, bg, bu, w_down, bd)


def _sc_split(n):
    per_w = n // SC_WORKERS
    assert per_w * SC_WORKERS == n and per_w % 8 == 0, n
    ch = max(c for c in range(8, SC_MAX_CHUNK + 1, 8) if per_w % c == 0)
    return per_w, ch


def _sc_positions(pos):
    per_w, ch = _sc_split(pos.shape[1])
    return pos.reshape(TOP_K, SC_WORKERS, per_w // ch, ch).transpose(1, 2, 0, 3)


def _sc_mesh():
    return plsc.VectorSubcoreMesh(core_axis_name="c", subcore_axis_name="s",
                                  num_cores=SC_CORES, num_subcores=SC_SUBCORES)


def _sc_worker():
    return lax.axis_index("s") * SC_CORES + lax.axis_index("c")


def _sc_dispatch(srcs, poss, p_rows):
    w = srcs[0].shape[1]
    geo = [_sc_split(s.shape[0]) for s in srcs]
    nseg = len(srcs)
    scratch = []
    for _, ch in geo:
        scratch += [pltpu.VMEM((TOP_K, ch), jnp.int32), pltpu.VMEM((ch, w), jnp.int32)]
    scratch.append(pltpu.SemaphoreType.DMA)

    def body(*refs):
        src_refs, pos_refs, xs_ref = refs[:nseg], refs[nseg:2 * nseg], refs[2 * nseg]
        scr = refs[2 * nseg + 1:]
        sem = scr[-1]
        wid = _sc_worker()
        for g in range(nseg):
            per_w, ch = geo[g]

            @pl.loop(0, per_w // ch)
            def _(c, src=src_refs[g], pos=pos_refs[g], idx_v=scr[2 * g], rows_v=scr[2 * g + 1],
                  per_w=per_w, ch=ch):
                base = pl.multiple_of(wid * per_w + c * ch, 8)
                pltpu.sync_copy(pos.at[wid, c], idx_v)
                pltpu.sync_copy(src.at[pl.ds(base, ch)], rows_v)
                copies = [pltpu.async_copy(rows_v, xs_ref.at[idx_v.at[kk]], sem) for kk in range(TOP_K)]
                for cp in copies:
                    cp.wait()

    return pl.kernel(body, out_type=jax.ShapeDtypeStruct((p_rows, w), jnp.int32), mesh=_sc_mesh(),
                     scratch_types=scratch, name="sc_dispatch")(*srcs, *poss)


def _sc_gather(ys, poss, seg_rows):
    w = ys.shape[1]
    geo = [_sc_split(n) for n in seg_rows]
    nseg = len(seg_rows)
    row0 = [sum(seg_rows[:g]) for g in range(nseg)]
    scratch = []
    for _, ch in geo:
        scratch += [pltpu.VMEM((TOP_K, ch), jnp.int32), pltpu.VMEM((TOP_K, ch, w), jnp.int32)]
    scratch += [pltpu.SemaphoreType.DMA, pltpu.SemaphoreType.DMA]

    def body(*refs):
        ys_ref, pos_refs, out_ref = refs[0], refs[1:1 + nseg], refs[1 + nseg]
        scr = refs[2 + nseg:]
        sem_g, sem_w = scr[-2], scr[-1]
        wid = _sc_worker()
        for g in range(nseg):
            per_w, ch = geo[g]

            @pl.loop(0, per_w // ch)
            def _(c, pos=pos_refs[g], idx_v=scr[2 * g], rows_v=scr[2 * g + 1], per_w=per_w, ch=ch,
                  r0=row0[g]):
                base = pl.multiple_of(r0 + wid * per_w + c * ch, 8)
                pltpu.sync_copy(pos.at[wid, c], idx_v)
                gathers = [pltpu.async_copy(ys_ref.at[idx_v.at[kk]], rows_v.at[kk], sem_g)
                           for kk in range(TOP_K)]
                writes = []
                for kk in range(TOP_K):
                    gathers[kk].wait()
                    writes.append(pltpu.async_copy(rows_v.at[kk], out_ref.at[kk, pl.ds(base, ch)], sem_w))
                for cp in writes:
                    cp.wait()

    return pl.kernel(body, out_type=jax.ShapeDtypeStruct((TOP_K, sum(seg_rows), w), jnp.int32),
                     mesh=_sc_mesh(), scratch_types=scratch, name="sc_gather")(ys, *poss)


def _final_kernel(h_ref, yg_ref, tw_ref, g_ref, o_ref):
    tw = tw_ref[...]
    acc = None
    for kk in range(TOP_K):
        lo, hi = _unpack_halves(yg_ref[kk])
        part = jnp.concatenate([lo, hi], axis=1) * tw[:, kk:kk + 1]
        acc = part if acc is None else acc + part
    h = h_ref[...] + acc
    ms = jnp.mean(h * h, axis=-1, keepdims=True)
    o_ref[...] = (h * lax.rsqrt(ms + EPS)) * g_ref[...]


def _final(h, yg, tw, g, row0):
    n, d = h.shape
    tm = min(n, STREAM_TILE)
    assert n % tm == 0 and row0 % tm == 0
    blk0 = row0 // tm
    return pl.pallas_call(
        _final_kernel,
        grid=(n // tm,),
        in_specs=[pl.BlockSpec((tm, d), lambda i: (i, 0)),
                  pl.BlockSpec((TOP_K, tm, d // 2), lambda i: (0, i + blk0, 0)),
                  pl.BlockSpec((tm, V_DIM), lambda i: (i, 0)),
                  _const_spec((1, d))],
        out_specs=pl.BlockSpec((tm, d), lambda i: (i, 0)),
        out_shape=jax.ShapeDtypeStruct((n, d), F32),
        compiler_params=_cparams(("arbitrary",), -(-2 * tm * (2 * d * 4 + TOP_K * d * 2 + V_DIM * 4) // MIB) + 6),
        name="final",
    )(h, yg, tw, g)


def kernel(x_prompt, x_sample, cache_k, cache_v, state_conv, g_attn_norm, w_in, lambda_q, lambda_k,
           g_subln, conv_w, w_attn_proj, w_conv_proj, w_out, g_ffn_norm, w_router, b_router, w_gu, b_gu,
           w_down, b_down, g_final):
    assert w_in.shape[0] == 1, "single-layer trunk"
    bp, tp, d = x_prompt.shape
    bs, ts, _ = x_sample.shape
    past = cache_k.shape[2]
    f = w_down.shape[2]
    n_p, n_s = bp * tp, bs * ts
    n_all = n_p + n_s

    w_in_b = w_in[0].astype(BF16)
    wcp_b = w_conv_proj[0].astype(BF16)
    wap_b = w_attn_proj[0].astype(BF16)
    wout_b = w_out[0].astype(BF16)
    wr_b = w_router[0].astype(BF16)
    bgu = b_gu[0].reshape(N_EXPERTS, 1, f, 2)
    bg, bu = bgu[..., 0], bgu[..., 1]
    bd = b_down[0].reshape(N_EXPERTS, 1, d)
    g_attn = g_attn_norm[0].reshape(1, d)
    g_ffn = g_ffn_norm[0].reshape(1, d)
    gs = g_subln[0].reshape(1, V_DIM)
    lq, lk = lambda_q[0], lambda_k[0]
    br = b_router[0].reshape(1, N_EXPERTS)

    prev_p = jnp.zeros((bp, 8, d), F32)
    prev_s = jnp.pad(state_conv[0], ((0, 0), (6, 0), (0, 0)))

    q_p, k_p, v_p, kb_p, vb_p, sga_p, gcc_p, conv_p = _inproj(x_prompt, prev_p, g_attn, w_in_b, conv_w[0], wcp_b)
    q_s, k_s, v_s, kb_s, vb_s, sga_s, gcc_s, conv_s = _inproj(x_sample, prev_s, g_attn, w_in_b, conv_w[0], wcp_b)

    on_p = _attn_prompt(lq, lk, gs, q_p, kb_p, vb_p)
    on_s = _attn_sample(lq, lk, gs, q_s, cache_k, cache_v, kb_s, vb_s)

    flat = lambda a, n: a.reshape(n, d)
    cnt0 = jnp.zeros((1, N_EXPERTS), F32)
    heads = lambda a: a.reshape(1, *a.shape[:2], N_HEADS, V_DIM)
    k_p, v_p, k_s, v_s = heads(k_p), heads(v_p), heads(k_s), heads(v_s)
    h_p, hn_p, tw_p, route_p, cnt_p = _post(cnt0, flat(x_prompt, n_p), flat(on_p, n_p), flat(sga_p, n_p),
                                            flat(gcc_p, n_p), wap_b, wout_b, g_ffn, wr_b, br, k_p, v_p)
    h_s, hn_s, tw_s, route_s, cnt = _post(cnt_p, flat(x_sample, n_s), flat(on_s, n_s), flat(sga_s, n_s),
                                          flat(gcc_s, n_s), wap_b, wout_b, g_ffn, wr_b, br, k_s, v_s)

    counts = cnt[0].astype(jnp.int32)
    padded = ((counts + MOE_TILE - 1) // MOE_TILE) * MOE_TILE
    ends = jnp.cumsum(padded)
    starts = ends - padded
    n_tiles = (n_all * TOP_K) // MOE_TILE + N_EXPERTS
    tile_row = jnp.arange(n_tiles, dtype=jnp.int32) * MOE_TILE
    tile_expert = jnp.sum((tile_row[:, None] >= ends[None, :]).astype(jnp.int32), axis=1)
    tile_expert = jnp.minimum(tile_expert, N_EXPERTS - 1)
    tile_valid = (tile_row < ends[-1]).astype(jnp.int32)

    def sorted_rows(route):
        experts = jnp.arange(N_EXPERTS, dtype=jnp.int32)[:, None, None]
        start = jnp.sum(jnp.where(route[None, :TOP_K] == experts, starts[:, None, None], 0), axis=0)
        return start + route[TOP_K:]

    poss = [_sc_positions(sorted_rows(route)) for route in (route_p, route_s)]
    xs = _sc_dispatch([hn_p, hn_s], poss, n_tiles * MOE_TILE)
    ys = _moe(tile_expert, tile_valid, xs, w_gu[0], bg, bu, w_down[0], bd)
    yg = _sc_gather(ys, poss, [n_p, n_s])

    y_p = _final(h_p, yg, tw_p, g_final.reshape(1, d), 0)
    y_s = _final(h_s, yg, tw_s, g_final.reshape(1, d), n_p)

    return (y_p.reshape(bp, tp, d), y_s.reshape(bs, ts, d), k_p, v_p, conv_p[None], k_s, v_s, conv_s[None])
```

```python
import functools
import math

import jax
import jax.numpy as jnp
from jax import lax
from jax.experimental import pallas as pl
from jax.experimental.pallas import tpu as pltpu
from jax.experimental.pallas import tpu_sc as plsc

N_HEADS = 8
HEAD_DIM = 64
V_DIM = 2 * HEAD_DIM
CHUNK = 64
N_EXPERTS = 32
TOP_K = 4
SWIGLU_LIMIT = 7.0
SWIGLU_ALPHA = 1.702
EPS = 1e-6
NEG = -1e30
LAMBDA_INIT = 0.8 - 0.6 * math.exp(-0.3 * 0)
N_SEG = 8
Q_SCALE = HEAD_DIM ** -0.5 * math.log2(math.e)

ROW_TILE = 256
STREAM_TILE = 1024
POST_CHUNK = 1024
ATTN_TILE = 1024
MOE_TILE = 1024
MOE_ROWS_STEP = 512
MIB = 1024 * 1024
SC_CORES = 2
SC_SUBCORES = 16
SC_WORKERS = SC_CORES * SC_SUBCORES
SC_MAX_CHUNK = 32

BF16 = jnp.bfloat16
F32 = jnp.float32


def _cparams(semantics, vmem_mib):
    return pltpu.CompilerParams(dimension_semantics=semantics, vmem_limit_bytes=vmem_mib * MIB)


def _const_spec(shape):
    nd = len(shape)
    return pl.BlockSpec(shape, lambda *_: (0,) * nd, pipeline_mode=pl.Buffered(1))


def _sigmoid(x):
    return 1.0 / (1.0 + jnp.exp(-x))


def _dot(a, b):
    return jnp.dot(a, b, preferred_element_type=F32)


def _dot_nt(a, b):
    return lax.dot_general(a, b, (((1,), (1,)), ((), ())), preferred_element_type=F32)


def _pack_halves(x):
    w = x.shape[1] // 2
    lo = lax.bitcast_convert_type(x[:, :w].astype(BF16).astype(F32), jnp.uint32)
    hi = lax.bitcast_convert_type(x[:, w:].astype(BF16).astype(F32), jnp.uint32)
    return lax.bitcast_convert_type((lo >> 16) | (hi & jnp.uint32(0xFFFF0000)), jnp.int32)


def _unpack_halves(words):
    u = lax.bitcast_convert_type(words, jnp.uint32)
    lo = lax.bitcast_convert_type(u << 16, F32)
    hi = lax.bitcast_convert_type(u & jnp.uint32(0xFFFF0000), F32)
    return lo, hi


def _inproj_kernel(x_ref, prev_ref, g_ref, w_ref, cw_ref, wcp_ref,
                   q_ref, k32_ref, v32_ref, kb_ref, vb_ref, sga_ref, gcc_ref, cnew_ref, ubuf):
    t = pl.program_id(1)
    tm, d = x_ref.shape[1], x_ref.shape[2]

    @pl.when(t == 0)
    def _():
        ubuf[0:8, :] = prev_ref[0]

    x = x_ref[0]
    ms = jnp.mean(x * x, axis=-1, keepdims=True)
    xn = ((x * lax.rsqrt(ms + EPS)) * g_ref[...]).astype(BF16)

    def seg(i):
        return _dot(xn, w_ref[:, i * d:(i + 1) * d])

    q_ref[0] = (seg(0) * Q_SCALE).astype(BF16)
    k = seg(1)
    k32_ref[0] = k
    kb_ref[0] = k.astype(BF16)
    v = seg(2)
    v32_ref[0] = v
    vb_ref[0] = v.astype(BF16)

    u = seg(5) * seg(3)
    ubuf[8:8 + tm, :] = u
    conv = (ubuf[6:6 + tm, :] * cw_ref[0:1, :] + ubuf[7:7 + tm, :] * cw_ref[1:2, :]
            + u * cw_ref[2:3, :])
    cpre = (seg(4) * conv).astype(BF16)
    c_out = _dot(cpre, wcp_ref[...])
    gcc_ref[0] = (_sigmoid(seg(7)) * c_out).astype(BF16)
    sga_ref[0] = _sigmoid(seg(6)).astype(BF16)
    cnew_ref[0] = ubuf[tm + 6:tm + 8, :]
    ubuf[0:8, :] = ubuf[tm:tm + 8, :]


def _inproj(x, conv_prev8, g, w_in_b, conv_w, wcp_b):
    b, t, d = x.shape
    tm = min(t, ROW_TILE)
    assert t % tm == 0 and tm % 8 == 0
    row = lambda: pl.BlockSpec((1, tm, d), lambda i, j: (i, j, 0))
    bf = jax.ShapeDtypeStruct((b, t, d), BF16)
    f32 = jax.ShapeDtypeStruct((b, t, d), F32)
    return pl.pallas_call(
        _inproj_kernel,
        grid=(b, t // tm),
        in_specs=[row(),
                  pl.BlockSpec((1, 8, d), lambda i, j: (i, 0, 0)),
                  _const_spec((1, d)),
                  _const_spec((d, N_SEG * d)),
                  _const_spec((3, d)),
                  _const_spec((d, d))],
        out_specs=[row(), row(), row(), row(), row(), row(), row(),
                   pl.BlockSpec((1, 2, d), lambda i, j: (i, 0, 0))],
        out_shape=[bf, f32, f32, bf, bf, bf, bf, jax.ShapeDtypeStruct((b, 2, d), F32)],
        scratch_shapes=[pltpu.VMEM((tm + 8, d), F32)],
        compiler_params=_cparams(("arbitrary", "arbitrary"), 56),
        name="inproj",
    )(x, conv_prev8, g, w_in_b, conv_w, wcp_b)


def _lambda(lq_ref, lk_ref):
    e = jnp.exp(jnp.sum(lq_ref[...] * lk_ref[...], axis=-1, keepdims=True))
    return e[0:1, :] - e[1:2, :] + LAMBDA_INIT


def _stack_maps(q):
    lane = lax.broadcasted_iota(jnp.int32, q.shape, 1)
    zero = jnp.zeros_like(q)
    return jnp.concatenate([jnp.where(lane < HEAD_DIM, q, zero),
                            jnp.where(lane >= HEAD_DIM, q, zero)], axis=0)


def _attn_finish(acc, l, lam, gs, tq):
    o = acc / l
    o = o[:tq] - lam * o[tq:]
    ms = jnp.mean(o * o, axis=-1, keepdims=True)
    return ((o * lax.rsqrt(ms + EPS)) * gs) * (1.0 - LAMBDA_INIT)


def _attn_prompt_kernel(lq_ref, lk_ref, gs_ref, q_ref, k_ref, v_ref, o_ref,
                        s_sc, p_sc, qq_sc, m_sc, l_sc, acc_sc, bias_sc, *, tile):
    n_q = q_ref.shape[1] // tile
    lane_tiles = [slice(c * V_DIM, (c + 1) * V_DIM) for c in range(tile // V_DIM)]
    lam = _lambda(lq_ref, lk_ref)

    def tile_rows(j):
        return pl.ds(j * tile if isinstance(j, int) else pl.multiple_of(j * tile, tile), tile)

    def rows_of(ref, j):
        return ref[0, tile_rows(j), :]

    @pl.when((pl.program_id(0) == 0) & (pl.program_id(1) == 0))
    def _():
        r = lax.broadcasted_iota(jnp.int32, bias_sc.shape, 0)
        c = lax.broadcasted_iota(jnp.int32, bias_sc.shape, 1)
        r = jnp.where(r >= tile, r - tile, r)
        bias_sc[...] = jnp.where((c // CHUNK) <= (r // CHUNK), 0.0, NEG).astype(F32)

    def key_tile(j, p):
        if isinstance(p, int):
            return j if p == 0 else p - 1
        return jnp.where(p == 0, j, p - 1)

    def diag_scores(j):
        qq = _stack_maps(rows_of(q_ref, j))
        qq_sc[...] = qq
        s_sc[...] = _dot_nt(qq, rows_of(k_ref, j)) + bias_sc[...]

    def scores(kt):
        s_sc[...] = _dot_nt(qq_sc[...], rows_of(k_ref, kt))

    def probs_v(kt):
        return _dot(p_sc[...], rows_of(v_ref, kt))

    def softmax(pv):
        m_cur = s_sc[:, lane_tiles[0]]
        for lt in lane_tiles[1:]:
            m_cur = jnp.maximum(m_cur, s_sc[:, lt])
        m_cur = jnp.max(m_cur, axis=-1, keepdims=True)
        if pv is None:
            m_new = jnp.broadcast_to(m_cur, m_sc.shape)
        else:
            m_old = m_sc[...]
            m_new = jnp.maximum(m_old, m_cur)
            alpha = jnp.exp2(m_old - m_new)
        psum = None
        for lt in lane_tiles:
            pc = jnp.exp2(s_sc[:, lt] - m_new)
            p_sc[:, lt] = pc.astype(BF16)
            psum = pc if psum is None else psum + pc
        if pv is None:
            l_sc[...] = psum
            acc_sc[...] = jnp.zeros(acc_sc.shape, F32)
        else:
            l_sc[...] = alpha * l_sc[...] + psum
            acc_sc[...] = alpha * (acc_sc[...] + pv)
        m_sc[...] = m_new

    def finish(j, last_kt):
        acc = acc_sc[...] + probs_v(last_kt)
        l = jnp.sum(l_sc[...], axis=-1, keepdims=True)
        o_ref[0, tile_rows(j), :] = _attn_finish(
            acc, l, lam, gs_ref[...], tile).astype(BF16)

    diag_scores(0)
    softmax(None)
    diag_scores(min(1, n_q - 1))

    def query_tile(j, carry):
        finish(j - 1, key_tile(j - 1, j - 1))
        softmax(None)
        scores(0)

        def body(p, c):
            pv = probs_v(key_tile(j, p - 1))
            softmax(pv)
            scores(p)
            return c

        lax.fori_loop(1, j, body, 0)
        pv = probs_v(key_tile(j, j - 1))
        softmax(pv)
        diag_scores(jnp.minimum(j + 1, n_q - 1))
        return carry

    lax.fori_loop(1, n_q, query_tile, 0)
    finish(n_q - 1, key_tile(n_q - 1, n_q - 1))


def _attn_vmem_mib(t, tile):
    rows = 2 * tile
    score_f32 = rows * tile * 4
    scratch = (2 * score_f32 + score_f32 // 2 + 3 * rows * V_DIM * 4
               + rows * V_DIM * 2)
    blocks = 2 * 4 * (t * V_DIM * 2)
    temporaries = score_f32
    return -(-(scratch + blocks + temporaries) // MIB) + 6


def _attn_prompt(lq, lk, gs, q, kb, vb):
    b, t, d = q.shape
    tile = min(t, ATTN_TILE)
    assert t % tile == 0 and tile % CHUNK == 0 and tile % V_DIM == 0
    spec = pl.BlockSpec((1, t, V_DIM), lambda bi, h: (bi, 0, h))
    rows = 2 * tile
    return pl.pallas_call(
        functools.partial(_attn_prompt_kernel, tile=tile),
        grid=(b, N_HEADS),
        in_specs=[_const_spec((2, HEAD_DIM)), _const_spec((2, HEAD_DIM)), _const_spec((1, V_DIM)),
                  spec, spec, spec],
        out_specs=spec,
        out_shape=jax.ShapeDtypeStruct((b, t, d), BF16),
        scratch_shapes=[pltpu.VMEM((rows, tile), F32), pltpu.VMEM((rows, tile), BF16),
                        pltpu.VMEM((rows, V_DIM), BF16),
                        pltpu.VMEM((rows, V_DIM), F32), pltpu.VMEM((rows, V_DIM), F32),
                        pltpu.VMEM((rows, V_DIM), F32), pltpu.VMEM((rows, tile), F32)],
        compiler_params=_cparams(("arbitrary", "arbitrary"), _attn_vmem_mib(t, tile)),
        name="attn_prompt",
    )(lq, lk, gs, q, kb, vb)


def _attn_sample_kernel(lq_ref, lk_ref, gs_ref, q_ref, ck_ref, cv_ref, kn_ref, vn_ref, o_ref):
    tq = q_ref.shape[1]
    lam = _lambda(lq_ref, lk_ref)
    for h in range(N_HEADS):
        cols = slice(h * V_DIM, (h + 1) * V_DIM)
        qq = _stack_maps(q_ref[0, :, cols])
        s_p = _dot_nt(qq, ck_ref[:, h, :].astype(BF16))
        s_n = _dot_nt(qq, kn_ref[0, :, cols])
        m = jnp.maximum(jnp.max(s_p, axis=-1, keepdims=True), jnp.max(s_n, axis=-1, keepdims=True))
        p_p = jnp.exp2(s_p - m)
        p_n = jnp.exp2(s_n - m)
        l = jnp.sum(p_p, axis=-1, keepdims=True) + jnp.sum(p_n, axis=-1, keepdims=True)
        acc = (_dot(p_p.astype(BF16), cv_ref[:, h, :].astype(BF16))
               + _dot(p_n.astype(BF16), vn_ref[0, :, cols]))
        o_ref[0, :, cols] = _attn_finish(acc, l, lam, gs_ref[...], tq).astype(BF16)


def _attn_sample(lq, lk, gs, q, cache_k, cache_v, kb, vb):
    b, t, d = q.shape
    past = cache_k.shape[2]
    qspec = pl.BlockSpec((1, t, d), lambda bi: (bi, 0, 0))
    cspec = pl.BlockSpec((None, None, past, N_HEADS, V_DIM), lambda bi: (0, bi, 0, 0, 0))
    return pl.pallas_call(
        _attn_sample_kernel,
        grid=(b,),
        in_specs=[_const_spec((2, HEAD_DIM)), _const_spec((2, HEAD_DIM)), _const_spec((1, V_DIM)),
                  qspec, cspec, cspec, qspec, qspec],
        out_specs=qspec,
        out_shape=jax.ShapeDtypeStruct((b, t, d), BF16),
        compiler_params=_cparams(("arbitrary",), 40),
        name="attn_sample",
    )(lq, lk, gs, q, cache_k, cache_v, kb, vb)


def _lane_cols(cols, dtype):
    rows = cols[0].shape[0]
    lane = lax.broadcasted_iota(jnp.int32, (rows, V_DIM), 1)
    out = jnp.zeros((rows, V_DIM), dtype)
    for kk, col in enumerate(cols):
        out = jnp.where(lane == kk, col.astype(dtype), out)
    return out


def _post_kernel(cnt_in_ref, x_ref, on_ref, sga_ref, gcc_ref, wap_ref, wout_ref, g_ref, wr_ref, br_ref,
                 k_out_ref, v_out_ref, h_ref, hn_ref, tw_ref, route_ref, cnt_ref, carry):
    tm = x_ref.shape[0]
    ch = min(tm, POST_CHUNK)

    @pl.when(pl.program_id(0) == 0)
    def _():
        carry[...] = cnt_in_ref[...]

    r = lax.broadcasted_iota(jnp.int32, (ch, ch), 0)
    c = lax.broadcasted_iota(jnp.int32, (ch, ch), 1)
    earlier = (c < r).astype(BF16)
    counts = carry[...]
    for c0 in range(0, tm, ch):
        rows = slice(c0, c0 + ch)
        a_out = _dot(on_ref[rows, :], wap_ref[...])
        merged = sga_ref[rows, :].astype(F32) * a_out + gcc_ref[rows, :].astype(F32)
        h = x_ref[rows, :] + _dot(merged.astype(BF16), wout_ref[...])
        h_ref[rows, :] = h
        ms = jnp.mean(h * h, axis=-1, keepdims=True)
        hn32 = (h * lax.rsqrt(ms + EPS)) * g_ref[...]
        hn_ref[rows, :] = _pack_halves(hn32)

        logits = _dot(hn32.astype(BF16), wr_ref[...]) + br_ref[...]
        lane = lax.broadcasted_iota(jnp.int32, logits.shape, 1)
        work = logits
        vals, idxs, sels = [], [], []
        for _ in range(TOP_K):
            mk = jnp.max(work, axis=-1, keepdims=True)
            ik = jnp.min(jnp.where(work == mk, lane, N_EXPERTS), axis=-1, keepdims=True)
            sel = lane == ik
            work = jnp.where(sel, -jnp.inf, work)
            vals.append(mk)
            idxs.append(ik)
            sels.append(sel)
        es = [jnp.exp(vk - vals[0]) for vk in vals]
        denom = es[0] + es[1] + es[2] + es[3]
        tw_ref[rows, :] = _lane_cols([e / denom for e in es], F32)

        picked = (sels[0] | sels[1] | sels[2] | sels[3])
        before = _dot(earlier, picked.astype(BF16)) + counts
        ranks = [jnp.sum(jnp.where(sk, before, 0.0), axis=-1, keepdims=True) for sk in sels]
        route = _lane_cols(idxs + ranks, jnp.int32)
        route_ref[:, rows] = route.T[0:2 * TOP_K, :]
        counts = counts + jnp.sum(picked.astype(F32), axis=0, keepdims=True)
    carry[...] = counts
    cnt_ref[...] = counts


def _post_vmem_mib(tm, d):
    blocks = 2 * (tm * d * (4 + 2 + 2 + 2 + 4 + 2) + tm * V_DIM * 4 + 2 * TOP_K * tm * 4)
    weights = 2 * d * d * 2 + d * N_EXPERTS * 2
    temporaries = 4 * tm * d * 4
    return -(-(blocks + weights + temporaries) // MIB) + 4


def _post(cnt_in, x, on, sga, gcc, wap_b, wout_b, g, wr_b, br, k_out, v_out):
    n, d = x.shape
    tm = min(n, STREAM_TILE)
    assert n % tm == 0 and tm % min(tm, POST_CHUNK) == 0
    row = lambda: pl.BlockSpec((tm, d), lambda i: (i, 0))
    cnt = lambda: pl.BlockSpec((1, N_EXPERTS), lambda i: (0, 0))
    return pl.pallas_call(
        _post_kernel,
        grid=(n // tm,),
        in_specs=[cnt(), row(), row(), row(), row(),
                  _const_spec((d, d)), _const_spec((d, d)), _const_spec((1, d)),
                  _const_spec((d, N_EXPERTS)), _const_spec((1, N_EXPERTS)),
                  pl.BlockSpec(memory_space=pl.ANY), pl.BlockSpec(memory_space=pl.ANY)],
        out_specs=[row(), pl.BlockSpec((tm, d // 2), lambda i: (i, 0)),
                   pl.BlockSpec((tm, V_DIM), lambda i: (i, 0)),
                   pl.BlockSpec((2 * TOP_K, tm), lambda i: (0, i)), cnt()],
        out_shape=[jax.ShapeDtypeStruct((n, d), F32), jax.ShapeDtypeStruct((n, d // 2), jnp.int32),
                   jax.ShapeDtypeStruct((n, V_DIM), F32),
                   jax.ShapeDtypeStruct((2 * TOP_K, n), jnp.int32),
                   jax.ShapeDtypeStruct((1, N_EXPERTS), F32)],
        scratch_shapes=[pltpu.VMEM((1, N_EXPERTS), F32)],
        compiler_params=_cparams(("arbitrary",), _post_vmem_mib(tm, d)),
        name="post",
    )(cnt_in, x, on, sga, gcc, wap_b, wout_b, g, wr_b, br, k_out, v_out)


def _moe_kernel(te_ref, tv_ref, x_ref, wgu_ref, bg_ref, bu_ref, wd_ref, bd_ref, y_ref, wgu_sc, wd_sc):
    t = pl.program_id(0)
    valid = tv_ref[t] > 0
    new_expert = (t == 0) | (te_ref[t] != te_ref[jnp.maximum(t - 1, 0)])
    f = wd_ref.shape[1]
    pair = 2 * V_DIM

    @pl.when(valid & new_expert)
    def _():
        r = lax.broadcasted_iota(jnp.int32, (pair, pair), 0)
        c = lax.broadcasted_iota(jnp.int32, (pair, pair), 1)
        src = jnp.where(c < V_DIM, 2 * c, 2 * (c - V_DIM) + 1)
        perm = (r == src).astype(BF16)
        for b in range(2 * f // pair):
            blk = wgu_ref[0, :, b * pair:(b + 1) * pair].astype(BF16)
            wgu_sc[:, b * pair:(b + 1) * pair] = _dot(blk, perm).astype(BF16)
        wd_sc[...] = wd_ref[0].astype(BF16)

    def expert_mlp(m):
        x_lo, x_hi = _unpack_halves(x_ref[0:m, :])
        x = jnp.concatenate([x_lo.astype(BF16), x_hi.astype(BF16)], axis=1)
        hgu = _dot(x, wgu_sc[...])
        acts = []
        for b in range(f // V_DIM):
            cols = slice(b * V_DIM, (b + 1) * V_DIM)
            gate = jnp.minimum(hgu[:, b * pair:b * pair + V_DIM] + bg_ref[0, :, cols], SWIGLU_LIMIT)
            up = jnp.clip(hgu[:, b * pair + V_DIM:(b + 1) * pair] + bu_ref[0, :, cols],
                          -SWIGLU_LIMIT, SWIGLU_LIMIT)
            glu = gate * _sigmoid(SWIGLU_ALPHA * gate)
            acts.append((glu * (up + 1.0)).astype(BF16))
        y = _dot(jnp.concatenate(acts, axis=1), wd_sc[...]) + bd_ref[0]
        y_ref[0:m, :] = _pack_halves(y)

    tile = x_ref.shape[0]
    for m in range(MOE_ROWS_STEP, tile + 1, MOE_ROWS_STEP):
        @pl.when((tv_ref[t] > m - MOE_ROWS_STEP) & (tv_ref[t] <= m))
        def _(m=m):
            expert_mlp(m)


def _moe_vmem_mib(tile, d, f):
    weights = 2 * (d * 2 * f + f * d) * 4 + (d * 2 * f + f * d) * 2
    rows = 2 * 2 * tile * (d // 2) * 4
    temporaries = tile * 2 * f * 4 + tile * f * 2 + tile * d * 2
    return -(-(weights + rows + temporaries) // MIB) + 4


def _moe(tile_expert, tile_valid, xs, w_gu, bg, bu, w_down, bd):
    p = xs.shape[0]
    f, d = w_down.shape[1:]
    n_tiles = p // MOE_TILE
    row = pl.BlockSpec((MOE_TILE, d // 2), lambda i, te, tv: (i, 0))
    wspec = lambda a, b_: pl.BlockSpec((1, a, b_), lambda i, te, tv: (te[i], 0, 0))
    return pl.pallas_call(
        _moe_kernel,
        grid_spec=pltpu.PrefetchScalarGridSpec(
            num_scalar_prefetch=2,
            grid=(n_tiles,),
            in_specs=[row, wspec(d, 2 * f), wspec(1, f), wspec(1, f), wspec(f, d), wspec(1, d)],
            out_specs=row,
            scratch_shapes=[pltpu.VMEM((d, 2 * f), BF16), pltpu.VMEM((f, d), BF16)]),
        out_shape=jax.ShapeDtypeStruct((p, d // 2), jnp.int32),
        compiler_params=_cparams(("arbitrary",), _moe_vmem_mib(MOE_TILE, d, f)),
        name="moe",
    )(tile_expert, tile_valid, xs, w_gu, bg, bu, w_down, bd)


def _sc_split(n):
    per_w = n // SC_WORKERS
    assert per_w * SC_WORKERS == n and per_w % 8 == 0, n
    ch = max(c for c in range(8, SC_MAX_CHUNK + 1, 8) if per_w % c == 0)
    return per_w, ch


def _sc_positions(pos):
    per_w, ch = _sc_split(pos.shape[1])
    return pos.reshape(TOP_K, SC_WORKERS, per_w // ch, ch).transpose(1, 2, 0, 3)


def _sc_mesh():
    return plsc.VectorSubcoreMesh(core_axis_name="c", subcore_axis_name="s",
                                  num_cores=SC_CORES, num_subcores=SC_SUBCORES)


def _sc_worker():
    return lax.axis_index("s") * SC_CORES + lax.axis_index("c")


def _sc_dispatch(srcs, poss, p_rows):
    w = srcs[0].shape[1]
    geo = [_sc_split(s.shape[0]) for s in srcs]
    nseg = len(srcs)
    scratch = []
    for _, ch in geo:
        scratch += [pltpu.VMEM((TOP_K, ch), jnp.int32), pltpu.VMEM((ch, w), jnp.int32)]
    scratch.append(pltpu.SemaphoreType.DMA)

    def body(*refs):
        src_refs, pos_refs, xs_ref = refs[:nseg], refs[nseg:2 * nseg], refs[2 * nseg]
        scr = refs[2 * nseg + 1:]
        sem = scr[-1]
        wid = _sc_worker()
        for g in range(nseg):
            per_w, ch = geo[g]

            @pl.loop(0, per_w // ch)
            def _(c, src=src_refs[g], pos=pos_refs[g], idx_v=scr[2 * g], rows_v=scr[2 * g + 1],
                  per_w=per_w, ch=ch):
                base = pl.multiple_of(wid * per_w + c * ch, 8)
                pltpu.sync_copy(pos.at[wid, c], idx_v)
                pltpu.sync_copy(src.at[pl.ds(base, ch)], rows_v)
                copies = [pltpu.async_copy(rows_v, xs_ref.at[idx_v.at[kk]], sem) for kk in range(TOP_K)]
                for cp in copies:
                    cp.wait()

    return pl.kernel(body, out_type=jax.ShapeDtypeStruct((p_rows, w), jnp.int32), mesh=_sc_mesh(),
                     scratch_types=scratch, name="sc_dispatch")(*srcs, *poss)


def _sc_gather(ys, poss, seg_rows):
    w = ys.shape[1]
    geo = [_sc_split(n) for n in seg_rows]
    nseg = len(seg_rows)
    row0 = [sum(seg_rows[:g]) for g in range(nseg)]
    scratch = []
    for _, ch in geo:
        scratch += [pltpu.VMEM((TOP_K, ch), jnp.int32), pltpu.VMEM((TOP_K, ch, w), jnp.int32)]
    scratch += [pltpu.SemaphoreType.DMA, pltpu.SemaphoreType.DMA]

    def body(*refs):
        ys_ref, pos_refs, out_ref = refs[0], refs[1:1 + nseg], refs[1 + nseg]
        scr = refs[2 + nseg:]
        sem_g, sem_w = scr[-2], scr[-1]
        wid = _sc_worker()
        for g in range(nseg):
            per_w, ch = geo[g]

            @pl.loop(0, per_w // ch)
            def _(c, pos=pos_refs[g], idx_v=scr[2 * g], rows_v=scr[2 * g + 1], per_w=per_w, ch=ch,
                  r0=row0[g]):
                base = pl.multiple_of(r0 + wid * per_w + c * ch, 8)
                pltpu.sync_copy(pos.at[wid, c], idx_v)
                gathers = [pltpu.async_copy(ys_ref.at[idx_v.at[kk]], rows_v.at[kk], sem_g)
                           for kk in range(TOP_K)]
                writes = []
                for kk in range(TOP_K):
                    gathers[kk].wait()
                    writes.append(pltpu.async_copy(rows_v.at[kk], out_ref.at[kk, pl.ds(base, ch)], sem_w))
                for cp in writes:
                    cp.wait()

    return pl.kernel(body, out_type=jax.ShapeDtypeStruct((TOP_K, sum(seg_rows), w), jnp.int32),
                     mesh=_sc_mesh(), scratch_types=scratch, name="sc_gather")(ys, *poss)


def _final_kernel(h_ref, yg_ref, tw_ref, g_ref, o_ref):
    tw = tw_ref[...]
    acc = None
    for kk in range(TOP_K):
        lo, hi = _unpack_halves(yg_ref[kk])
        part = jnp.concatenate([lo, hi], axis=1) * tw[:, kk:kk + 1]
        acc = part if acc is None else acc + part
    h = h_ref[...] + acc
    ms = jnp.mean(h * h, axis=-1, keepdims=True)
    o_ref[...] = (h * lax.rsqrt(ms + EPS)) * g_ref[...]


def _final(h, yg, tw, g, row0):
    n, d = h.shape
    tm = min(n, STREAM_TILE)
    assert n % tm == 0 and row0 % tm == 0
    blk0 = row0 // tm
    return pl.pallas_call(
        _final_kernel,
        grid=(n // tm,),
        in_specs=[pl.BlockSpec((tm, d), lambda i: (i, 0)),
                  pl.BlockSpec((TOP_K, tm, d // 2), lambda i: (0, i + blk0, 0)),
                  pl.BlockSpec((tm, V_DIM), lambda i: (i, 0)),
                  _const_spec((1, d))],
        out_specs=pl.BlockSpec((tm, d), lambda i: (i, 0)),
        out_shape=jax.ShapeDtypeStruct((n, d), F32),
        compiler_params=_cparams(("arbitrary",), -(-2 * tm * (2 * d * 4 + TOP_K * d * 2 + V_DIM * 4) // MIB) + 6),
        name="final",
    )(h, yg, tw, g)


def kernel(x_prompt, x_sample, cache_k, cache_v, state_conv, g_attn_norm, w_in, lambda_q, lambda_k,
           g_subln, conv_w, w_attn_proj, w_conv_proj, w_out, g_ffn_norm, w_router, b_router, w_gu, b_gu,
           w_down, b_down, g_final):
    assert w_in.shape[0] == 1, "single-layer trunk"
    bp, tp, d = x_prompt.shape
    bs, ts, _ = x_sample.shape
    past = cache_k.shape[2]
    f = w_down.shape[2]
    n_p, n_s = bp * tp, bs * ts
    n_all = n_p + n_s

    w_in_b = w_in[0].astype(BF16)
    wcp_b = w_conv_proj[0].astype(BF16)
    wap_b = w_attn_proj[0].astype(BF16)
    wout_b = w_out[0].astype(BF16)
    wr_b = w_router[0].astype(BF16)
    bgu = b_gu[0].reshape(N_EXPERTS, 1, f, 2)
    bg, bu = bgu[..., 0], bgu[..., 1]
    bd = b_down[0].reshape(N_EXPERTS, 1, d)
    g_attn = g_attn_norm[0].reshape(1, d)
    g_ffn = g_ffn_norm[0].reshape(1, d)
    gs = g_subln[0].reshape(1, V_DIM)
    lq, lk = lambda_q[0], lambda_k[0]
    br = b_router[0].reshape(1, N_EXPERTS)

    prev_p = jnp.zeros((bp, 8, d), F32)
    prev_s = jnp.pad(state_conv[0], ((0, 0), (6, 0), (0, 0)))

    q_p, k_p, v_p, kb_p, vb_p, sga_p, gcc_p, conv_p = _inproj(x_prompt, prev_p, g_attn, w_in_b, conv_w[0], wcp_b)
    q_s, k_s, v_s, kb_s, vb_s, sga_s, gcc_s, conv_s = _inproj(x_sample, prev_s, g_attn, w_in_b, conv_w[0], wcp_b)

    on_p = _attn_prompt(lq, lk, gs, q_p, kb_p, vb_p)
    on_s = _attn_sample(lq, lk, gs, q_s, cache_k, cache_v, kb_s, vb_s)

    flat = lambda a, n: a.reshape(n, d)
    cnt0 = jnp.zeros((1, N_EXPERTS), F32)
    heads = lambda a: a.reshape(1, *a.shape[:2], N_HEADS, V_DIM)
    k_p, v_p, k_s, v_s = heads(k_p), heads(v_p), heads(k_s), heads(v_s)
    h_p, hn_p, tw_p, route_p, cnt_p = _post(cnt0, flat(x_prompt, n_p), flat(on_p, n_p), flat(sga_p, n_p),
                                            flat(gcc_p, n_p), wap_b, wout_b, g_ffn, wr_b, br, k_p, v_p)
    h_s, hn_s, tw_s, route_s, cnt = _post(cnt_p, flat(x_sample, n_s), flat(on_s, n_s), flat(sga_s, n_s),
                                          flat(gcc_s, n_s), wap_b, wout_b, g_ffn, wr_b, br, k_s, v_s)

    counts = cnt[0].astype(jnp.int32)
    padded = ((counts + MOE_TILE - 1) // MOE_TILE) * MOE_TILE
    ends = jnp.cumsum(padded)
    starts = ends - padded
    n_tiles = (n_all * TOP_K) // MOE_TILE + N_EXPERTS
    tile_row = jnp.arange(n_tiles, dtype=jnp.int32) * MOE_TILE
    tile_expert = jnp.sum((tile_row[:, None] >= ends[None, :]).astype(jnp.int32), axis=1)
    tile_expert = jnp.minimum(tile_expert, N_EXPERTS - 1)
    group_end = jnp.sum(jnp.where(tile_expert[:, None] == jnp.arange(N_EXPERTS, dtype=jnp.int32)[None, :],
                                  (starts + counts)[None, :], 0), axis=1)
    tile_valid = jnp.clip(group_end - tile_row, 0, MOE_TILE)
    tile_valid = jnp.where(tile_row < ends[-1], tile_valid, 0).astype(jnp.int32)

    def sorted_rows(route):
        experts = jnp.arange(N_EXPERTS, dtype=jnp.int32)[:, None, None]
        start = jnp.sum(jnp.where(route[None, :TOP_K] == experts, starts[:, None, None], 0), axis=0)
        return start + route[TOP_K:]

    poss = [_sc_positions(sorted_rows(route)) for route in (route_p, route_s)]
    xs = _sc_dispatch([hn_p, hn_s], poss, n_tiles * MOE_TILE)
    ys = _moe(tile_expert, tile_valid, xs, w_gu[0], bg, bu, w_down[0], bd)
    yg = _sc_gather(ys, poss, [n_p, n_s])

    y_p = _final(h_p, yg, tw_p, g_final.reshape(1, d), 0)
    y_s = _final(h_s, yg, tw_s, g_final.reshape(1, d), n_p)

    return (y_p.reshape(bp, tp, d), y_s.reshape(bs, ts, d), k_p, v_p, conv_p[None], k_s, v_s, conv_s[None])
```

```python
import functools
import math

import jax
import jax.numpy as jnp
from jax import lax
from jax.experimental import pallas as pl
from jax.experimental.pallas import tpu as pltpu
from jax.experimental.pallas import tpu_sc as plsc

N_HEADS = 8
HEAD_DIM = 64
V_DIM = 2 * HEAD_DIM
CHUNK = 64
N_EXPERTS = 32
TOP_K = 4
SWIGLU_LIMIT = 7.0
SWIGLU_ALPHA = 1.702
EPS = 1e-6
NEG = -1e30
LAMBDA_INIT = 0.8 - 0.6 * math.exp(-0.3 * 0)
N_SEG = 8
Q_SCALE = HEAD_DIM ** -0.5 * math.log2(math.e)

ROW_TILE = 256
STREAM_TILE = 1024
ATTN_TILE = 1024
MOE_TILE = 1024
MOE_ROWS_STEP = 512
MIB = 1024 * 1024
SC_CORES = 2
SC_SUBCORES = 16
SC_WORKERS = SC_CORES * SC_SUBCORES
SC_MAX_CHUNK = 32

BF16 = jnp.bfloat16
F32 = jnp.float32


def _cparams(semantics, vmem_mib):
    return pltpu.CompilerParams(dimension_semantics=semantics, vmem_limit_bytes=vmem_mib * MIB)


def _const_spec(shape):
    nd = len(shape)
    return pl.BlockSpec(shape, lambda *_: (0,) * nd, pipeline_mode=pl.Buffered(1))


def _sigmoid(x):
    return 1.0 / (1.0 + jnp.exp(-x))


def _dot(a, b):
    return jnp.dot(a, b, preferred_element_type=F32)


def _dot_nt(a, b):
    return lax.dot_general(a, b, (((1,), (1,)), ((), ())), preferred_element_type=F32)


def _pack_halves(x):
    w = x.shape[1] // 2
    lo = lax.bitcast_convert_type(x[:, :w].astype(BF16).astype(F32), jnp.uint32)
    hi = lax.bitcast_convert_type(x[:, w:].astype(BF16).astype(F32), jnp.uint32)
    return lax.bitcast_convert_type((lo >> 16) | (hi & jnp.uint32(0xFFFF0000)), jnp.int32)


def _unpack_halves(words):
    u = lax.bitcast_convert_type(words, jnp.uint32)
    lo = lax.bitcast_convert_type(u << 16, F32)
    hi = lax.bitcast_convert_type(u & jnp.uint32(0xFFFF0000), F32)
    return lo, hi


def _inproj_kernel(x_ref, prev_ref, g_ref, w_ref, cw_ref, wcp_ref,
                   q_ref, k32_ref, v32_ref, kb_ref, vb_ref, sga_ref, gcc_ref, cnew_ref, ubuf):
    t = pl.program_id(1)
    tm, d = x_ref.shape[1], x_ref.shape[2]

    @pl.when(t == 0)
    def _():
        ubuf[0:8, :] = prev_ref[0]

    x = x_ref[0]
    ms = jnp.mean(x * x, axis=-1, keepdims=True)
    xn = ((x * lax.rsqrt(ms + EPS)) * g_ref[...]).astype(BF16)

    def seg(i):
        return _dot(xn, w_ref[:, i * d:(i + 1) * d])

    q_ref[0] = (seg(0) * Q_SCALE).astype(BF16)
    k = seg(1)
    k32_ref[0] = k
    kb_ref[0] = k.astype(BF16)
    v = seg(2)
    v32_ref[0] = v
    vb_ref[0] = v.astype(BF16)

    u = seg(5) * seg(3)
    ubuf[8:8 + tm, :] = u
    conv = (ubuf[6:6 + tm, :] * cw_ref[0:1, :] + ubuf[7:7 + tm, :] * cw_ref[1:2, :]
            + u * cw_ref[2:3, :])
    cpre = (seg(4) * conv).astype(BF16)
    c_out = _dot(cpre, wcp_ref[...])
    gcc_ref[0] = (_sigmoid(seg(7)) * c_out).astype(BF16)
    sga_ref[0] = _sigmoid(seg(6)).astype(BF16)
    cnew_ref[0] = ubuf[tm + 6:tm + 8, :]
    ubuf[0:8, :] = ubuf[tm:tm + 8, :]


def _inproj(x, conv_prev8, g, w_in_b, conv_w, wcp_b):
    b, t, d = x.shape
    tm = min(t, ROW_TILE)
    assert t % tm == 0 and tm % 8 == 0
    row = lambda: pl.BlockSpec((1, tm, d), lambda i, j: (i, j, 0))
    bf = jax.ShapeDtypeStruct((b, t, d), BF16)
    f32 = jax.ShapeDtypeStruct((b, t, d), F32)
    return pl.pallas_call(
        _inproj_kernel,
        grid=(b, t // tm),
        in_specs=[row(),
                  pl.BlockSpec((1, 8, d), lambda i, j: (i, 0, 0)),
                  _const_spec((1, d)),
                  _const_spec((d, N_SEG * d)),
                  _const_spec((3, d)),
                  _const_spec((d, d))],
        out_specs=[row(), row(), row(), row(), row(), row(), row(),
                   pl.BlockSpec((1, 2, d), lambda i, j: (i, 0, 0))],
        out_shape=[bf, f32, f32, bf, bf, bf, bf, jax.ShapeDtypeStruct((b, 2, d), F32)],
        scratch_shapes=[pltpu.VMEM((tm + 8, d), F32)],
        compiler_params=_cparams(("arbitrary", "arbitrary"), 56),
        name="inproj",
    )(x, conv_prev8, g, w_in_b, conv_w, wcp_b)


def _lambda(lq_ref, lk_ref):
    e = jnp.exp(jnp.sum(lq_ref[...] * lk_ref[...], axis=-1, keepdims=True))
    return e[0:1, :] - e[1:2, :] + LAMBDA_INIT


def _stack_maps(q):
    lane = lax.broadcasted_iota(jnp.int32, q.shape, 1)
    zero = jnp.zeros_like(q)
    return jnp.concatenate([jnp.where(lane < HEAD_DIM, q, zero),
                            jnp.where(lane >= HEAD_DIM, q, zero)], axis=0)


def _attn_finish(acc, l, lam, gs, tq):
    o = acc / l
    o = o[:tq] - lam * o[tq:]
    ms = jnp.mean(o * o, axis=-1, keepdims=True)
    return ((o * lax.rsqrt(ms + EPS)) * gs) * (1.0 - LAMBDA_INIT)


def _attn_prompt_kernel(lq_ref, lk_ref, gs_ref, q_ref, k_ref, v_ref, o_ref,
                        s_sc, p_sc, qq_sc, m_sc, l_sc, acc_sc, bias_sc, *, tile):
    n_q = q_ref.shape[1] // tile
    lane_tiles = [slice(c * V_DIM, (c + 1) * V_DIM) for c in range(tile // V_DIM)]
    lam = _lambda(lq_ref, lk_ref)

    def tile_rows(j):
        return pl.ds(j * tile if isinstance(j, int) else pl.multiple_of(j * tile, tile), tile)

    def rows_of(ref, j):
        return ref[0, tile_rows(j), :]

    @pl.when((pl.program_id(0) == 0) & (pl.program_id(1) == 0))
    def _():
        r = lax.broadcasted_iota(jnp.int32, bias_sc.shape, 0)
        c = lax.broadcasted_iota(jnp.int32, bias_sc.shape, 1)
        r = jnp.where(r >= tile, r - tile, r)
        bias_sc[...] = jnp.where((c // CHUNK) <= (r // CHUNK), 0.0, NEG).astype(F32)

    def key_tile(j, p):
        if isinstance(p, int):
            return j if p == 0 else p - 1
        return jnp.where(p == 0, j, p - 1)

    def diag_scores(j):
        qq = _stack_maps(rows_of(q_ref, j))
        qq_sc[...] = qq
        s_sc[...] = _dot_nt(qq, rows_of(k_ref, j)) + bias_sc[...]

    def scores(kt):
        s_sc[...] = _dot_nt(qq_sc[...], rows_of(k_ref, kt))

    def probs_v(kt):
        return _dot(p_sc[...], rows_of(v_ref, kt))

    def softmax(pv):
        m_cur = s_sc[:, lane_tiles[0]]
        for lt in lane_tiles[1:]:
            m_cur = jnp.maximum(m_cur, s_sc[:, lt])
        m_cur = jnp.max(m_cur, axis=-1, keepdims=True)
        if pv is None:
            m_new = jnp.broadcast_to(m_cur, m_sc.shape)
        else:
            m_old = m_sc[...]
            m_new = jnp.maximum(m_old, m_cur)
            alpha = jnp.exp2(m_old - m_new)
        psum = None
        for lt in lane_tiles:
            pc = jnp.exp2(s_sc[:, lt] - m_new)
            p_sc[:, lt] = pc.astype(BF16)
            psum = pc if psum is None else psum + pc
        if pv is None:
            l_sc[...] = psum
            acc_sc[...] = jnp.zeros(acc_sc.shape, F32)
        else:
            l_sc[...] = alpha * l_sc[...] + psum
            acc_sc[...] = alpha * (acc_sc[...] + pv)
        m_sc[...] = m_new

    def finish(j, last_kt):
        acc = acc_sc[...] + probs_v(last_kt)
        l = jnp.sum(l_sc[...], axis=-1, keepdims=True)
        o_ref[0, tile_rows(j), :] = _attn_finish(
            acc, l, lam, gs_ref[...], tile).astype(BF16)

    diag_scores(0)
    softmax(None)
    diag_scores(min(1, n_q - 1))

    def query_tile(j, carry):
        finish(j - 1, key_tile(j - 1, j - 1))
        softmax(None)
        scores(0)

        def body(p, c):
            pv = probs_v(key_tile(j, p - 1))
            softmax(pv)
            scores(p)
            return c

        lax.fori_loop(1, j, body, 0)
        pv = probs_v(key_tile(j, j - 1))
        softmax(pv)
        diag_scores(jnp.minimum(j + 1, n_q - 1))
        return carry

    lax.fori_loop(1, n_q, query_tile, 0)
    finish(n_q - 1, key_tile(n_q - 1, n_q - 1))


def _attn_vmem_mib(t, tile):
    rows = 2 * tile
    score_f32 = rows * tile * 4
    scratch = (2 * score_f32 + score_f32 // 2 + 3 * rows * V_DIM * 4
               + rows * V_DIM * 2)
    blocks = 2 * 4 * (t * V_DIM * 2)
    temporaries = score_f32
    return -(-(scratch + blocks + temporaries) // MIB) + 6


def _attn_prompt(lq, lk, gs, q, kb, vb):
    b, t, d = q.shape
    tile = min(t, ATTN_TILE)
    assert t % tile == 0 and tile % CHUNK == 0 and tile % V_DIM == 0
    spec = pl.BlockSpec((1, t, V_DIM), lambda bi, h: (bi, 0, h))
    rows = 2 * tile
    return pl.pallas_call(
        functools.partial(_attn_prompt_kernel, tile=tile),
        grid=(b, N_HEADS),
        in_specs=[_const_spec((2, HEAD_DIM)), _const_spec((2, HEAD_DIM)), _const_spec((1, V_DIM)),
                  spec, spec, spec],
        out_specs=spec,
        out_shape=jax.ShapeDtypeStruct((b, t, d), BF16),
        scratch_shapes=[pltpu.VMEM((rows, tile), F32), pltpu.VMEM((rows, tile), BF16),
                        pltpu.VMEM((rows, V_DIM), BF16),
                        pltpu.VMEM((rows, V_DIM), F32), pltpu.VMEM((rows, V_DIM), F32),
                        pltpu.VMEM((rows, V_DIM), F32), pltpu.VMEM((rows, tile), F32)],
        compiler_params=_cparams(("arbitrary", "arbitrary"), _attn_vmem_mib(t, tile)),
        name="attn_prompt",
    )(lq, lk, gs, q, kb, vb)


def _attn_sample_kernel(lq_ref, lk_ref, gs_ref, q_ref, ck_ref, cv_ref, kn_ref, vn_ref, o_ref):
    tq = q_ref.shape[1]
    lam = _lambda(lq_ref, lk_ref)
    heads = [slice(h * V_DIM, (h + 1) * V_DIM) for h in range(N_HEADS)]
    qqs = [_stack_maps(q_ref[0, :, cols]) for cols in heads]
    s_ps = [_dot_nt(qq, ck_ref[:, h, :].astype(BF16)) for h, qq in enumerate(qqs)]
    s_ns = [_dot_nt(qq, kn_ref[0, :, cols]) for qq, cols in zip(qqs, heads)]
    ms = [jnp.maximum(jnp.max(s_p, axis=-1, keepdims=True), jnp.max(s_n, axis=-1, keepdims=True))
          for s_p, s_n in zip(s_ps, s_ns)]
    p_ps = [jnp.exp2(s_p - m) for s_p, m in zip(s_ps, ms)]
    p_ns = [jnp.exp2(s_n - m) for s_n, m in zip(s_ns, ms)]
    ls = [jnp.sum(p_p, axis=-1, keepdims=True) + jnp.sum(p_n, axis=-1, keepdims=True)
          for p_p, p_n in zip(p_ps, p_ns)]
    accs = [_dot(p_p.astype(BF16), cv_ref[:, h, :].astype(BF16)) + _dot(p_n.astype(BF16), vn_ref[0, :, cols])
            for h, (p_p, p_n, cols) in enumerate(zip(p_ps, p_ns, heads))]
    for cols, acc, l in zip(heads, accs, ls):
        o_ref[0, :, cols] = _attn_finish(acc, l, lam, gs_ref[...], tq).astype(BF16)


def _attn_sample(lq, lk, gs, q, cache_k, cache_v, kb, vb):
    b, t, d = q.shape
    past = cache_k.shape[2]
    qspec = pl.BlockSpec((1, t, d), lambda bi: (bi, 0, 0))
    cspec = pl.BlockSpec((None, None, past, N_HEADS, V_DIM), lambda bi: (0, bi, 0, 0, 0))
    return pl.pallas_call(
        _attn_sample_kernel,
        grid=(b,),
        in_specs=[_const_spec((2, HEAD_DIM)), _const_spec((2, HEAD_DIM)), _const_spec((1, V_DIM)),
                  qspec, cspec, cspec, qspec, qspec],
        out_specs=qspec,
        out_shape=jax.ShapeDtypeStruct((b, t, d), BF16),
        compiler_params=_cparams(("arbitrary",), 40),
        name="attn_sample",
    )(lq, lk, gs, q, cache_k, cache_v, kb, vb)


def _lane_cols(cols, dtype):
    rows = cols[0].shape[0]
    lane = lax.broadcasted_iota(jnp.int32, (rows, V_DIM), 1)
    out = jnp.zeros((rows, V_DIM), dtype)
    for kk, col in enumerate(cols):
        out = jnp.where(lane == kk, col.astype(dtype), out)
    return out


def _post_kernel(cnt_in_ref, x_ref, on_ref, sga_ref, gcc_ref, wap_ref, wout_ref, g_ref, wr_ref, br_ref,
                 k_out_ref, v_out_ref, h_ref, hn_ref, tw_ref, route_ref, cnt_ref, carry, logits_sc):
    i = pl.program_id(0)
    tm = x_ref.shape[0]

    @pl.when(i == 0)
    def _():
        carry[...] = cnt_in_ref[...]
        logits_sc[...] = jnp.zeros(logits_sc.shape, F32)

    logits = logits_sc[...]
    a_out = _dot(on_ref[...], wap_ref[...])

    lane = lax.broadcasted_iota(jnp.int32, logits.shape, 1)
    work = logits
    vals, idxs, sels = [], [], []
    for _ in range(TOP_K):
        mk = jnp.max(work, axis=-1, keepdims=True)
        ik = jnp.min(jnp.where(work == mk, lane, N_EXPERTS), axis=-1, keepdims=True)
        sel = lane == ik
        work = jnp.where(sel, -jnp.inf, work)
        vals.append(mk)
        idxs.append(ik)
        sels.append(sel)
    es = [jnp.exp(vk - vals[0]) for vk in vals]
    denom = es[0] + es[1] + es[2] + es[3]
    tw_ref[...] = _lane_cols([e / denom for e in es], F32)

    merged = sga_ref[...].astype(F32) * a_out + gcc_ref[...].astype(F32)
    h = x_ref[...] + _dot(merged.astype(BF16), wout_ref[...])
    h_ref[...] = h

    picked = (sels[0] | sels[1] | sels[2] | sels[3]).astype(BF16)
    r = lax.broadcasted_iota(jnp.int32, (tm, tm), 0)
    c = lax.broadcasted_iota(jnp.int32, (tm, tm), 1)
    counts = carry[...]
    before = _dot((c < r).astype(BF16), picked) + counts
    ranks = [jnp.sum(jnp.where(sk, before, 0.0), axis=-1, keepdims=True) for sk in sels]
    route = _lane_cols(idxs + ranks, jnp.int32)
    route_ref[...] = route.T[0:2 * TOP_K, :]
    routed = jnp.where(i > 0, 1.0, 0.0)
    counts = counts + routed * jnp.sum(picked.astype(F32), axis=0, keepdims=True)
    carry[...] = counts
    cnt_ref[...] = counts

    ms = jnp.mean(h * h, axis=-1, keepdims=True)
    hn32 = (h * lax.rsqrt(ms + EPS)) * g_ref[...]
    hn_ref[...] = _pack_halves(hn32)
    logits_sc[...] = _dot(hn32.astype(BF16), wr_ref[...]) + br_ref[...]


def _post_vmem_mib(tm, d):
    blocks = 2 * (tm * d * (4 + 2 + 2 + 2 + 4 + 2) + tm * V_DIM * 4 + 2 * TOP_K * tm * 4)
    weights = 2 * d * d * 2 + d * N_EXPERTS * 2
    temporaries = 4 * tm * d * 4
    return -(-(blocks + weights + temporaries) // MIB) + 4


def _post(cnt_in, x, on, sga, gcc, wap_b, wout_b, g, wr_b, br, k_out, v_out):
    n, d = x.shape
    tm = min(n, STREAM_TILE)
    assert n % tm == 0
    n_tiles = n // tm
    this = lambda i: (jnp.minimum(i, n_tiles - 1), 0)
    prev = lambda i: (jnp.maximum(i - 1, 0), 0)
    row = lambda: pl.BlockSpec((tm, d), this)
    cnt = lambda: pl.BlockSpec((1, N_EXPERTS), lambda i: (0, 0))
    return pl.pallas_call(
        _post_kernel,
        grid=(n_tiles + 1,),
        in_specs=[cnt(), row(), row(), row(), row(),
                  _const_spec((d, d)), _const_spec((d, d)), _const_spec((1, d)),
                  _const_spec((d, N_EXPERTS)), _const_spec((1, N_EXPERTS)),
                  pl.BlockSpec(memory_space=pl.ANY), pl.BlockSpec(memory_space=pl.ANY)],
        out_specs=[row(), pl.BlockSpec((tm, d // 2), this),
                   pl.BlockSpec((tm, V_DIM), prev),
                   pl.BlockSpec((2 * TOP_K, tm), lambda i: (0, jnp.maximum(i - 1, 0))), cnt()],
        out_shape=[jax.ShapeDtypeStruct((n, d), F32), jax.ShapeDtypeStruct((n, d // 2), jnp.int32),
                   jax.ShapeDtypeStruct((n, V_DIM), F32),
                   jax.ShapeDtypeStruct((2 * TOP_K, n), jnp.int32),
                   jax.ShapeDtypeStruct((1, N_EXPERTS), F32)],
        scratch_shapes=[pltpu.VMEM((1, N_EXPERTS), F32), pltpu.VMEM((tm, N_EXPERTS), F32)],
        compiler_params=_cparams(("arbitrary",), _post_vmem_mib(tm, d)),
        name="post",
    )(cnt_in, x, on, sga, gcc, wap_b, wout_b, g, wr_b, br, k_out, v_out)


def _moe_kernel(te_ref, tv_ref, x_ref, wgu_ref, bg_ref, bu_ref, wd_ref, bd_ref, y_ref, wgu_sc, wd_sc):
    t = pl.program_id(0)
    valid = tv_ref[t] > 0
    new_expert = (t == 0) | (te_ref[t] != te_ref[jnp.maximum(t - 1, 0)])
    f = wd_ref.shape[1]
    pair = 2 * V_DIM

    @pl.when(valid & new_expert)
    def _():
        r = lax.broadcasted_iota(jnp.int32, (pair, pair), 0)
        c = lax.broadcasted_iota(jnp.int32, (pair, pair), 1)
        src = jnp.where(c < V_DIM, 2 * c, 2 * (c - V_DIM) + 1)
        perm = (r == src).astype(BF16)
        for b in range(2 * f // pair):
            blk = wgu_ref[0, :, b * pair:(b + 1) * pair].astype(BF16)
            wgu_sc[:, b * pair:(b + 1) * pair] = _dot(blk, perm).astype(BF16)
        wd_sc[...] = wd_ref[0].astype(BF16)

    def expert_mlp(m):
        x_lo, x_hi = _unpack_halves(x_ref[0:m, :])
        x = jnp.concatenate([x_lo.astype(BF16), x_hi.astype(BF16)], axis=1)
        hgu = _dot(x, wgu_sc[...])
        acts = []
        for b in range(f // V_DIM):
            cols = slice(b * V_DIM, (b + 1) * V_DIM)
            gate = jnp.minimum(hgu[:, b * pair:b * pair + V_DIM] + bg_ref[0, :, cols], SWIGLU_LIMIT)
            up = jnp.clip(hgu[:, b * pair + V_DIM:(b + 1) * pair] + bu_ref[0, :, cols],
                          -SWIGLU_LIMIT, SWIGLU_LIMIT)
            glu = gate * _sigmoid(SWIGLU_ALPHA * gate)
            acts.append((glu * (up + 1.0)).astype(BF16))
        y = _dot(jnp.concatenate(acts, axis=1), wd_sc[...]) + bd_ref[0]
        y_ref[0:m, :] = _pack_halves(y)

    tile = x_ref.shape[0]
    for m in range(MOE_ROWS_STEP, tile + 1, MOE_ROWS_STEP):
        @pl.when((tv_ref[t] > m - MOE_ROWS_STEP) & (tv_ref[t] <= m))
        def _(m=m):
            expert_mlp(m)


def _moe_vmem_mib(tile, d, f):
    weights = 2 * (d * 2 * f + f * d) * 4 + (d * 2 * f + f * d) * 2
    rows = 2 * 2 * tile * (d // 2) * 4
    temporaries = tile * 2 * f * 4 + tile * f * 2 + tile * d * 2
    return -(-(weights + rows + temporaries) // MIB) + 4


def _moe(tile_expert, tile_valid, xs, w_gu, bg, bu, w_down, bd):
    p = xs.shape[0]
    f, d = w_down.shape[1:]
    n_tiles = p // MOE_TILE
    row = pl.BlockSpec((MOE_TILE, d // 2), lambda i, te, tv: (i, 0))
    wspec = lambda a, b_: pl.BlockSpec((1, a, b_), lambda i, te, tv: (te[i], 0, 0))
    return pl.pallas_call(
        _moe_kernel,
        grid_spec=pltpu.PrefetchScalarGridSpec(
            num_scalar_prefetch=2,
            grid=(n_tiles,),
            in_specs=[row, wspec(d, 2 * f), wspec(1, f), wspec(1, f), wspec(f, d), wspec(1, d)],
            out_specs=row,
            scratch_shapes=[pltpu.VMEM((d, 2 * f), BF16), pltpu.VMEM((f, d), BF16)]),
        out_shape=jax.ShapeDtypeStruct((p, d // 2), jnp.int32),
        compiler_params=_cparams(("arbitrary",), _moe_vmem_mib(MOE_TILE, d, f)),
        name="moe",
    )(tile_expert, tile_valid, xs, w_gu, bg, bu, w_down, bd)


def _sc_split(n):
    per_w = n // SC_WORKERS
    assert per_w * SC_WORKERS == n and per_w % 8 == 0, n
    ch = max(c for c in range(8, SC_MAX_CHUNK + 1, 8) if per_w % c == 0)
    return per_w, ch


def _sc_positions(pos):
    per_w, ch = _sc_split(pos.shape[1])
    return pos.reshape(TOP_K, SC_WORKERS, per_w // ch, ch).transpose(1, 2, 0, 3)


def _sc_mesh():
    return plsc.VectorSubcoreMesh(core_axis_name="c", subcore_axis_name="s",
                                  num_cores=SC_CORES, num_subcores=SC_SUBCORES)


def _sc_worker():
    return lax.axis_index("s") * SC_CORES + lax.axis_index("c")


def _sc_dispatch(srcs, poss, p_rows):
    w = srcs[0].shape[1]
    geo = [_sc_split(s.shape[0]) for s in srcs]
    nseg = len(srcs)
    scratch = []
    for _, ch in geo:
        scratch += [pltpu.VMEM((TOP_K, ch), jnp.int32), pltpu.VMEM((ch, w), jnp.int32)]
    scratch.append(pltpu.SemaphoreType.DMA)

    def body(*refs):
        src_refs, pos_refs, xs_ref = refs[:nseg], refs[nseg:2 * nseg], refs[2 * nseg]
        scr = refs[2 * nseg + 1:]
        sem = scr[-1]
        wid = _sc_worker()
        for g in range(nseg):
            per_w, ch = geo[g]

            @pl.loop(0, per_w // ch)
            def _(c, src=src_refs[g], pos=pos_refs[g], idx_v=scr[2 * g], rows_v=scr[2 * g + 1],
                  per_w=per_w, ch=ch):
                base = pl.multiple_of(wid * per_w + c * ch, 8)
                pltpu.sync_copy(pos.at[wid, c], idx_v)
                pltpu.sync_copy(src.at[pl.ds(base, ch)], rows_v)
                copies = [pltpu.async_copy(rows_v, xs_ref.at[idx_v.at[kk]], sem) for kk in range(TOP_K)]
                for cp in copies:
                    cp.wait()

    return pl.kernel(body, out_type=jax.ShapeDtypeStruct((p_rows, w), jnp.int32), mesh=_sc_mesh(),
                     scratch_types=scratch, name="sc_dispatch")(*srcs, *poss)


def _sc_gather(ys, poss, seg_rows):
    w = ys.shape[1]
    geo = [_sc_split(n) for n in seg_rows]
    nseg = len(seg_rows)
    row0 = [sum(seg_rows[:g]) for g in range(nseg)]
    scratch = []
    for _, ch in geo:
        scratch += [pltpu.VMEM((TOP_K, ch), jnp.int32), pltpu.VMEM((TOP_K, ch, w), jnp.int32)]
    scratch += [pltpu.SemaphoreType.DMA, pltpu.SemaphoreType.DMA]

    def body(*refs):
        ys_ref, pos_refs, out_ref = refs[0], refs[1:1 + nseg], refs[1 + nseg]
        scr = refs[2 + nseg:]
        sem_g, sem_w = scr[-2], scr[-1]
        wid = _sc_worker()
        for g in range(nseg):
            per_w, ch = geo[g]

            @pl.loop(0, per_w // ch)
            def _(c, pos=pos_refs[g], idx_v=scr[2 * g], rows_v=scr[2 * g + 1], per_w=per_w, ch=ch,
                  r0=row0[g]):
                base = pl.multiple_of(r0 + wid * per_w + c * ch, 8)
                pltpu.sync_copy(pos.at[wid, c], idx_v)
                gathers = [pltpu.async_copy(ys_ref.at[idx_v.at[kk]], rows_v.at[kk], sem_g)
                           for kk in range(TOP_K)]
                writes = []
                for kk in range(TOP_K):
                    gathers[kk].wait()
                    writes.append(pltpu.async_copy(rows_v.at[kk], out_ref.at[kk, pl.ds(base, ch)], sem_w))
                for cp in writes:
                    cp.wait()

    return pl.kernel(body, out_type=jax.ShapeDtypeStruct((TOP_K, sum(seg_rows), w), jnp.int32),
                     mesh=_sc_mesh(), scratch_types=scratch, name="sc_gather")(ys, *poss)


def _final_kernel(h_ref, yg_ref, tw_ref, g_ref, o_ref):
    tw = tw_ref[...]
    acc = None
    for kk in range(TOP_K):
        lo, hi = _unpack_halves(yg_ref[kk])
        part = jnp.concatenate([lo, hi], axis=1) * tw[:, kk:kk + 1]
        acc = part if acc is None else acc + part
    h = h_ref[...] + acc
    ms = jnp.mean(h * h, axis=-1, keepdims=True)
    o_ref[...] = (h * lax.rsqrt(ms + EPS)) * g_ref[...]


def _final(h, yg, tw, g, row0):
    n, d = h.shape
    tm = min(n, STREAM_TILE)
    assert n % tm == 0 and row0 % tm == 0
    blk0 = row0 // tm
    return pl.pallas_call(
        _final_kernel,
        grid=(n // tm,),
        in_specs=[pl.BlockSpec((tm, d), lambda i: (i, 0)),
                  pl.BlockSpec((TOP_K, tm, d // 2), lambda i: (0, i + blk0, 0)),
                  pl.BlockSpec((tm, V_DIM), lambda i: (i, 0)),
                  _const_spec((1, d))],
        out_specs=pl.BlockSpec((tm, d), lambda i: (i, 0)),
        out_shape=jax.ShapeDtypeStruct((n, d), F32),
        compiler_params=_cparams(("arbitrary",), -(-2 * tm * (2 * d * 4 + TOP_K * d * 2 + V_DIM * 4) // MIB) + 6),
        name="final",
    )(h, yg, tw, g)


def kernel(x_prompt, x_sample, cache_k, cache_v, state_conv, g_attn_norm, w_in, lambda_q, lambda_k,
           g_subln, conv_w, w_attn_proj, w_conv_proj, w_out, g_ffn_norm, w_router, b_router, w_gu, b_gu,
           w_down, b_down, g_final):
    assert w_in.shape[0] == 1, "single-layer trunk"
    bp, tp, d = x_prompt.shape
    bs, ts, _ = x_sample.shape
    past = cache_k.shape[2]
    f = w_down.shape[2]
    n_p, n_s = bp * tp, bs * ts
    n_all = n_p + n_s

    w_in_b = w_in[0].astype(BF16)
    wcp_b = w_conv_proj[0].astype(BF16)
    wap_b = w_attn_proj[0].astype(BF16)
    wout_b = w_out[0].astype(BF16)
    wr_b = w_router[0].astype(BF16)
    bgu = b_gu[0].reshape(N_EXPERTS, 1, f, 2)
    bg, bu = bgu[..., 0], bgu[..., 1]
    bd = b_down[0].reshape(N_EXPERTS, 1, d)
    g_attn = g_attn_norm[0].reshape(1, d)
    g_ffn = g_ffn_norm[0].reshape(1, d)
    gs = g_subln[0].reshape(1, V_DIM)
    lq, lk = lambda_q[0], lambda_k[0]
    br = b_router[0].reshape(1, N_EXPERTS)

    prev_p = jnp.zeros((bp, 8, d), F32)
    prev_s = jnp.pad(state_conv[0], ((0, 0), (6, 0), (0, 0)))

    q_p, k_p, v_p, kb_p, vb_p, sga_p, gcc_p, conv_p = _inproj(x_prompt, prev_p, g_attn, w_in_b, conv_w[0], wcp_b)
    q_s, k_s, v_s, kb_s, vb_s, sga_s, gcc_s, conv_s = _inproj(x_sample, prev_s, g_attn, w_in_b, conv_w[0], wcp_b)

    on_p = _attn_prompt(lq, lk, gs, q_p, kb_p, vb_p)
    on_s = _attn_sample(lq, lk, gs, q_s, cache_k, cache_v, kb_s, vb_s)

    flat = lambda a, n: a.reshape(n, d)
    cnt0 = jnp.zeros((1, N_EXPERTS), F32)
    heads = lambda a: a.reshape(1, *a.shape[:2], N_HEADS, V_DIM)
    k_p, v_p, k_s, v_s = heads(k_p), heads(v_p), heads(k_s), heads(v_s)
    h_p, hn_p, tw_p, route_p, cnt_p = _post(cnt0, flat(x_prompt, n_p), flat(on_p, n_p), flat(sga_p, n_p),
                                            flat(gcc_p, n_p), wap_b, wout_b, g_ffn, wr_b, br, k_p, v_p)
    h_s, hn_s, tw_s, route_s, cnt = _post(cnt_p, flat(x_sample, n_s), flat(on_s, n_s), flat(sga_s, n_s),
                                          flat(gcc_s, n_s), wap_b, wout_b, g_ffn, wr_b, br, k_s, v_s)

    counts = cnt[0].astype(jnp.int32)
    padded = ((counts + MOE_TILE - 1) // MOE_TILE) * MOE_TILE
    ends = jnp.cumsum(padded)
    starts = ends - padded
    n_tiles = (n_all * TOP_K) // MOE_TILE + N_EXPERTS
    tile_row = jnp.arange(n_tiles, dtype=jnp.int32) * MOE_TILE
    tile_expert = jnp.sum((tile_row[:, None] >= ends[None, :]).astype(jnp.int32), axis=1)
    tile_expert = jnp.minimum(tile_expert, N_EXPERTS - 1)
    group_end = jnp.sum(jnp.where(tile_expert[:, None] == jnp.arange(N_EXPERTS, dtype=jnp.int32)[None, :],
                                  (starts + counts)[None, :], 0), axis=1)
    tile_valid = jnp.clip(group_end - tile_row, 0, MOE_TILE)
    tile_valid = jnp.where(tile_row < ends[-1], tile_valid, 0).astype(jnp.int32)

    def sorted_rows(route):
        experts = jnp.arange(N_EXPERTS, dtype=jnp.int32)[:, None, None]
        start = jnp.sum(jnp.where(route[None, :TOP_K] == experts, starts[:, None, None], 0), axis=0)
        return start + route[TOP_K:]

    poss = [_sc_positions(sorted_rows(route)) for route in (route_p, route_s)]
    xs = _sc_dispatch([hn_p, hn_s], poss, n_tiles * MOE_TILE)
    ys = _moe(tile_expert, tile_valid, xs, w_gu[0], bg, bu, w_down[0], bd)
    yg = _sc_gather(ys, poss, [n_p, n_s])

    y_p = _final(h_p, yg, tw_p, g_final.reshape(1, d), 0)
    y_s = _final(h_s, yg, tw_s, g_final.reshape(1, d), n_p)

    return (y_p.reshape(bp, tp, d), y_s.reshape(bs, ts, d), k_p, v_p, conv_p[None], k_s, v_s, conv_s[None])
```

```python
import functools
import math

import jax
import jax.numpy as jnp
from jax import lax
from jax.experimental import pallas as pl
from jax.experimental.pallas import tpu as pltpu
from jax.experimental.pallas import tpu_sc as plsc

N_HEADS = 8
HEAD_DIM = 64
V_DIM = 2 * HEAD_DIM
CHUNK = 64
N_EXPERTS = 32
TOP_K = 4
SWIGLU_LIMIT = 7.0
SWIGLU_ALPHA = 1.702
EPS = 1e-6
NEG = -1e30
LAMBDA_INIT = 0.8 - 0.6 * math.exp(-0.3 * 0)
N_SEG = 8
Q_SCALE = HEAD_DIM ** -0.5 * math.log2(math.e)

ROW_TILE = 512
STREAM_TILE = 1024
ATTN_TILE = 1024
MOE_TILE = 1024
MOE_ROWS_STEP = 512
MIB = 1024 * 1024
SC_CORES = 2
SC_SUBCORES = 16
SC_WORKERS = SC_CORES * SC_SUBCORES
SC_MAX_CHUNK = 32

BF16 = jnp.bfloat16
F32 = jnp.float32


def _cparams(semantics, vmem_mib):
    return pltpu.CompilerParams(dimension_semantics=semantics, vmem_limit_bytes=vmem_mib * MIB)


def _const_spec(shape):
    nd = len(shape)
    return pl.BlockSpec(shape, lambda *_: (0,) * nd, pipeline_mode=pl.Buffered(1))


def _sigmoid(x):
    return 1.0 / (1.0 + jnp.exp(-x))


def _dot(a, b):
    return jnp.dot(a, b, preferred_element_type=F32)


def _dot_nt(a, b):
    return lax.dot_general(a, b, (((1,), (1,)), ((), ())), preferred_element_type=F32)


def _pack_halves(x):
    w = x.shape[1] // 2
    lo = lax.bitcast_convert_type(x[:, :w].astype(BF16).astype(F32), jnp.uint32)
    hi = lax.bitcast_convert_type(x[:, w:].astype(BF16).astype(F32), jnp.uint32)
    return lax.bitcast_convert_type((lo >> 16) | (hi & jnp.uint32(0xFFFF0000)), jnp.int32)


def _unpack_halves(words):
    u = lax.bitcast_convert_type(words, jnp.uint32)
    lo = lax.bitcast_convert_type(u << 16, F32)
    hi = lax.bitcast_convert_type(u & jnp.uint32(0xFFFF0000), F32)
    return lo, hi


def _inproj_kernel(x_ref, prev_ref, g_ref, w_ref, cw_ref, wcp_ref,
                   q_ref, k32_ref, v32_ref, kb_ref, vb_ref, sga_ref, gcc_ref, cnew_ref, ubuf):
    t = pl.program_id(1)
    tm, d = x_ref.shape[1], x_ref.shape[2]
    n_str = prev_ref.shape[1] // 8
    rows_s = tm // n_str
    base = [s * (rows_s + 8) for s in range(n_str)]

    @pl.when(t == 0)
    def _():
        for s in range(n_str):
            ubuf[base[s]:base[s] + 8, :] = prev_ref[0, 8 * s:8 * s + 8, :]

    x = x_ref[0]
    ms = jnp.mean(x * x, axis=-1, keepdims=True)
    xn = ((x * lax.rsqrt(ms + EPS)) * g_ref[...]).astype(BF16)

    def seg(i):
        return _dot(xn, w_ref[:, i * d:(i + 1) * d])

    q_ref[0] = (seg(0) * Q_SCALE).astype(BF16)
    k = seg(1)
    k32_ref[0] = k
    kb_ref[0] = k.astype(BF16)
    v = seg(2)
    v32_ref[0] = v
    vb_ref[0] = v.astype(BF16)

    u = seg(5) * seg(3)
    convs = []
    for s in range(n_str):
        u_s = u[s * rows_s:(s + 1) * rows_s]
        ubuf[base[s] + 8:base[s] + 8 + rows_s, :] = u_s
        convs.append(ubuf[base[s] + 6:base[s] + 6 + rows_s, :] * cw_ref[0:1, :]
                     + ubuf[base[s] + 7:base[s] + 7 + rows_s, :] * cw_ref[1:2, :] + u_s * cw_ref[2:3, :])
    conv = convs[0] if n_str == 1 else jnp.concatenate(convs, axis=0)
    cpre = (seg(4) * conv).astype(BF16)
    c_out = _dot(cpre, wcp_ref[...])
    gcc_ref[0] = (_sigmoid(seg(7)) * c_out).astype(BF16)
    sga_ref[0] = _sigmoid(seg(6)).astype(BF16)
    for s in range(n_str):
        cnew_ref[0, 2 * s:2 * s + 2, :] = ubuf[base[s] + rows_s + 6:base[s] + rows_s + 8, :]
        ubuf[base[s]:base[s] + 8, :] = ubuf[base[s] + rows_s:base[s] + rows_s + 8, :]


def _inproj(x, conv_prev8, g, w_in_b, conv_w, wcp_b):
    b, t, d = x.shape
    n_str = conv_prev8.shape[1] // 8
    tm = min(t, ROW_TILE)
    assert t % tm == 0 and tm % (8 * n_str) == 0 and (n_str == 1 or tm == t)
    row = lambda: pl.BlockSpec((1, tm, d), lambda i, j: (i, j, 0))
    bf = jax.ShapeDtypeStruct((b, t, d), BF16)
    f32 = jax.ShapeDtypeStruct((b, t, d), F32)
    return pl.pallas_call(
        _inproj_kernel,
        grid=(b, t // tm),
        in_specs=[row(),
                  pl.BlockSpec((1, 8 * n_str, d), lambda i, j: (i, 0, 0)),
                  _const_spec((1, d)),
                  _const_spec((d, N_SEG * d)),
                  _const_spec((3, d)),
                  _const_spec((d, d))],
        out_specs=[row(), row(), row(), row(), row(), row(), row(),
                   pl.BlockSpec((1, 2 * n_str, d), lambda i, j: (i, 0, 0))],
        out_shape=[bf, f32, f32, bf, bf, bf, bf, jax.ShapeDtypeStruct((b, 2 * n_str, d), F32)],
        scratch_shapes=[pltpu.VMEM((tm + 8 * n_str, d), F32)],
        compiler_params=_cparams(("arbitrary", "arbitrary"), 56),
        name="inproj",
    )(x, conv_prev8, g, w_in_b, conv_w, wcp_b)


def _lambda(lq_ref, lk_ref):
    e = jnp.exp(jnp.sum(lq_ref[...] * lk_ref[...], axis=-1, keepdims=True))
    return e[0:1, :] - e[1:2, :] + LAMBDA_INIT


def _stack_maps(q):
    lane = lax.broadcasted_iota(jnp.int32, q.shape, 1)
    zero = jnp.zeros_like(q)
    return jnp.concatenate([jnp.where(lane < HEAD_DIM, q, zero),
                            jnp.where(lane >= HEAD_DIM, q, zero)], axis=0)


def _attn_finish(acc, l, lam, gs, tq):
    o = acc / l
    o = o[:tq] - lam * o[tq:]
    ms = jnp.mean(o * o, axis=-1, keepdims=True)
    return ((o * lax.rsqrt(ms + EPS)) * gs) * (1.0 - LAMBDA_INIT)


def _attn_prompt_kernel(lq_ref, lk_ref, gs_ref, q_ref, k_ref, v_ref, o_ref,
                        s_sc, p_sc, qq_sc, m_sc, l_sc, acc_sc, bias_sc, *, tile):
    n_q = q_ref.shape[1] // tile
    lane_tiles = [slice(c * V_DIM, (c + 1) * V_DIM) for c in range(tile // V_DIM)]
    lam = _lambda(lq_ref, lk_ref)

    def tile_rows(j):
        return pl.ds(j * tile if isinstance(j, int) else pl.multiple_of(j * tile, tile), tile)

    def rows_of(ref, j):
        return ref[0, tile_rows(j), :]

    @pl.when((pl.program_id(0) == 0) & (pl.program_id(1) == 0))
    def _():
        r = lax.broadcasted_iota(jnp.int32, bias_sc.shape, 0)
        c = lax.broadcasted_iota(jnp.int32, bias_sc.shape, 1)
        r = jnp.where(r >= tile, r - tile, r)
        bias_sc[...] = jnp.where((c // CHUNK) <= (r // CHUNK), 0.0, NEG).astype(F32)

    def key_tile(j, p):
        if isinstance(p, int):
            return j if p == 0 else p - 1
        return jnp.where(p == 0, j, p - 1)

    def diag_scores(j):
        qq = _stack_maps(rows_of(q_ref, j))
        qq_sc[...] = qq
        s_sc[...] = _dot_nt(qq, rows_of(k_ref, j)) + bias_sc[...]

    def scores(kt):
        s_sc[...] = _dot_nt(qq_sc[...], rows_of(k_ref, kt))

    def probs_v(kt):
        return _dot(p_sc[...], rows_of(v_ref, kt))

    def softmax(pv):
        m_cur = s_sc[:, lane_tiles[0]]
        for lt in lane_tiles[1:]:
            m_cur = jnp.maximum(m_cur, s_sc[:, lt])
        m_cur = jnp.max(m_cur, axis=-1, keepdims=True)
        if pv is None:
            m_new = jnp.broadcast_to(m_cur, m_sc.shape)
        else:
            m_old = m_sc[...]
            m_new = jnp.maximum(m_old, m_cur)
            alpha = jnp.exp2(m_old - m_new)
        psum = None
        for lt in lane_tiles:
            pc = jnp.exp2(s_sc[:, lt] - m_new)
            p_sc[:, lt] = pc.astype(BF16)
            psum = pc if psum is None else psum + pc
        if pv is None:
            l_sc[...] = psum
            acc_sc[...] = jnp.zeros(acc_sc.shape, F32)
        else:
            l_sc[...] = alpha * l_sc[...] + psum
            acc_sc[...] = alpha * (acc_sc[...] + pv)
        m_sc[...] = m_new

    def finish(j, last_kt):
        acc = acc_sc[...] + probs_v(last_kt)
        l = jnp.sum(l_sc[...], axis=-1, keepdims=True)
        o_ref[0, tile_rows(j), :] = _attn_finish(
            acc, l, lam, gs_ref[...], tile).astype(BF16)

    diag_scores(0)
    softmax(None)
    diag_scores(min(1, n_q - 1))

    def query_tile(j, carry):
        finish(j - 1, key_tile(j - 1, j - 1))
        softmax(None)
        scores(0)

        def body(p, c):
            pv = probs_v(key_tile(j, p - 1))
            softmax(pv)
            scores(p)
            return c

        lax.fori_loop(1, j, body, 0)
        pv = probs_v(key_tile(j, j - 1))
        softmax(pv)
        diag_scores(jnp.minimum(j + 1, n_q - 1))
        return carry

    lax.fori_loop(1, n_q, query_tile, 0)
    finish(n_q - 1, key_tile(n_q - 1, n_q - 1))


def _attn_vmem_mib(t, tile):
    rows = 2 * tile
    score_f32 = rows * tile * 4
    scratch = (2 * score_f32 + score_f32 // 2 + 3 * rows * V_DIM * 4
               + rows * V_DIM * 2)
    blocks = 2 * 4 * (t * V_DIM * 2)
    temporaries = score_f32
    return -(-(scratch + blocks + temporaries) // MIB) + 6


def _attn_prompt(lq, lk, gs, q, kb, vb):
    b, t, d = q.shape
    tile = min(t, ATTN_TILE)
    assert t % tile == 0 and tile % CHUNK == 0 and tile % V_DIM == 0
    spec = pl.BlockSpec((1, t, V_DIM), lambda bi, h: (bi, 0, h))
    rows = 2 * tile
    return pl.pallas_call(
        functools.partial(_attn_prompt_kernel, tile=tile),
        grid=(b, N_HEADS),
        in_specs=[_const_spec((2, HEAD_DIM)), _const_spec((2, HEAD_DIM)), _const_spec((1, V_DIM)),
                  spec, spec, spec],
        out_specs=spec,
        out_shape=jax.ShapeDtypeStruct((b, t, d), BF16),
        scratch_shapes=[pltpu.VMEM((rows, tile), F32), pltpu.VMEM((rows, tile), BF16),
                        pltpu.VMEM((rows, V_DIM), BF16),
                        pltpu.VMEM((rows, V_DIM), F32), pltpu.VMEM((rows, V_DIM), F32),
                        pltpu.VMEM((rows, V_DIM), F32), pltpu.VMEM((rows, tile), F32)],
        compiler_params=_cparams(("arbitrary", "arbitrary"), _attn_vmem_mib(t, tile)),
        name="attn_prompt",
    )(lq, lk, gs, q, kb, vb)


def _attn_sample_kernel(lq_ref, lk_ref, gs_ref, q_ref, ck_ref, cv_ref, kn_ref, vn_ref, o_ref):
    tq = q_ref.shape[1]
    lam = _lambda(lq_ref, lk_ref)
    heads = [slice(h * V_DIM, (h + 1) * V_DIM) for h in range(N_HEADS)]
    qqs = [_stack_maps(q_ref[0, :, cols]) for cols in heads]
    s_ps = [_dot_nt(qq, ck_ref[:, h, :].astype(BF16)) for h, qq in enumerate(qqs)]
    s_ns = [_dot_nt(qq, kn_ref[0, :, cols]) for qq, cols in zip(qqs, heads)]
    ms = [jnp.maximum(jnp.max(s_p, axis=-1, keepdims=True), jnp.max(s_n, axis=-1, keepdims=True))
          for s_p, s_n in zip(s_ps, s_ns)]
    p_ps = [jnp.exp2(s_p - m) for s_p, m in zip(s_ps, ms)]
    p_ns = [jnp.exp2(s_n - m) for s_n, m in zip(s_ns, ms)]
    ls = [jnp.sum(p_p, axis=-1, keepdims=True) + jnp.sum(p_n, axis=-1, keepdims=True)
          for p_p, p_n in zip(p_ps, p_ns)]
    accs = [_dot(p_p.astype(BF16), cv_ref[:, h, :].astype(BF16)) + _dot(p_n.astype(BF16), vn_ref[0, :, cols])
            for h, (p_p, p_n, cols) in enumerate(zip(p_ps, p_ns, heads))]
    for cols, acc, l in zip(heads, accs, ls):
        o_ref[0, :, cols] = _attn_finish(acc, l, lam, gs_ref[...], tq).astype(BF16)


def _attn_sample(lq, lk, gs, q, cache_k, cache_v, kb, vb):
    b, t, d = q.shape
    past = cache_k.shape[2]
    qspec = pl.BlockSpec((1, t, d), lambda bi: (bi, 0, 0))
    cspec = pl.BlockSpec((None, None, past, N_HEADS, V_DIM), lambda bi: (0, bi, 0, 0, 0))
    return pl.pallas_call(
        _attn_sample_kernel,
        grid=(b,),
        in_specs=[_const_spec((2, HEAD_DIM)), _const_spec((2, HEAD_DIM)), _const_spec((1, V_DIM)),
                  qspec, cspec, cspec, qspec, qspec],
        out_specs=qspec,
        out_shape=jax.ShapeDtypeStruct((b, t, d), BF16),
        compiler_params=_cparams(("arbitrary",), 40),
        name="attn_sample",
    )(lq, lk, gs, q, cache_k, cache_v, kb, vb)


def _lane_cols(cols, dtype):
    rows = cols[0].shape[0]
    lane = lax.broadcasted_iota(jnp.int32, (rows, V_DIM), 1)
    out = jnp.zeros((rows, V_DIM), dtype)
    for kk, col in enumerate(cols):
        out = jnp.where(lane == kk, col.astype(dtype), out)
    return out


def _post_kernel(cnt_in_ref, x_ref, on_ref, sga_ref, gcc_ref, wap_ref, wout_ref, g_ref, wr_ref, br_ref,
                 k_out_ref, v_out_ref, h_ref, hn_ref, tw_ref, route_ref, cnt_ref, carry, logits_sc):
    i = pl.program_id(0)
    tm = x_ref.shape[0]

    @pl.when(i == 0)
    def _():
        carry[...] = cnt_in_ref[...]
        logits_sc[...] = jnp.zeros(logits_sc.shape, F32)

    logits = logits_sc[...]
    a_out = _dot(on_ref[...], wap_ref[...])

    lane = lax.broadcasted_iota(jnp.int32, logits.shape, 1)
    work = logits
    vals, idxs, sels = [], [], []
    for _ in range(TOP_K):
        mk = jnp.max(work, axis=-1, keepdims=True)
        ik = jnp.min(jnp.where(work == mk, lane, N_EXPERTS), axis=-1, keepdims=True)
        sel = lane == ik
        work = jnp.where(sel, -jnp.inf, work)
        vals.append(mk)
        idxs.append(ik)
        sels.append(sel)
    es = [jnp.exp(vk - vals[0]) for vk in vals]
    denom = es[0] + es[1] + es[2] + es[3]
    tw_ref[...] = _lane_cols([e / denom for e in es], F32)

    merged = sga_ref[...].astype(F32) * a_out + gcc_ref[...].astype(F32)
    h = x_ref[...] + _dot(merged.astype(BF16), wout_ref[...])
    h_ref[...] = h

    picked = (sels[0] | sels[1] | sels[2] | sels[3]).astype(BF16)
    r = lax.broadcasted_iota(jnp.int32, (tm, tm), 0)
    c = lax.broadcasted_iota(jnp.int32, (tm, tm), 1)
    counts = carry[...]
    before = _dot((c < r).astype(BF16), picked) + counts
    ranks = [jnp.sum(jnp.where(sk, before, 0.0), axis=-1, keepdims=True) for sk in sels]
    route = _lane_cols(idxs + ranks, jnp.int32)
    route_ref[...] = route.T[0:2 * TOP_K, :]
    routed = jnp.where(i > 0, 1.0, 0.0)
    counts = counts + routed * jnp.sum(picked.astype(F32), axis=0, keepdims=True)
    carry[...] = counts
    cnt_ref[...] = counts

    ms = jnp.mean(h * h, axis=-1, keepdims=True)
    hn32 = (h * lax.rsqrt(ms + EPS)) * g_ref[...]
    hn_ref[...] = _pack_halves(hn32)
    logits_sc[...] = _dot(hn32.astype(BF16), wr_ref[...]) + br_ref[...]


def _post_vmem_mib(tm, d):
    blocks = 2 * (tm * d * (4 + 2 + 2 + 2 + 4 + 2) + tm * V_DIM * 4 + 2 * TOP_K * tm * 4)
    weights = 2 * d * d * 2 + d * N_EXPERTS * 2
    temporaries = 4 * tm * d * 4
    return -(-(blocks + weights + temporaries) // MIB) + 4


def _post(cnt_in, x, on, sga, gcc, wap_b, wout_b, g, wr_b, br, k_out, v_out):
    n, d = x.shape
    tm = min(n, STREAM_TILE)
    assert n % tm == 0
    n_tiles = n // tm
    this = lambda i: (jnp.minimum(i, n_tiles - 1), 0)
    prev = lambda i: (jnp.maximum(i - 1, 0), 0)
    row = lambda: pl.BlockSpec((tm, d), this)
    cnt = lambda: pl.BlockSpec((1, N_EXPERTS), lambda i: (0, 0))
    return pl.pallas_call(
        _post_kernel,
        grid=(n_tiles + 1,),
        in_specs=[cnt(), row(), row(), row(), row(),
                  _const_spec((d, d)), _const_spec((d, d)), _const_spec((1, d)),
                  _const_spec((d, N_EXPERTS)), _const_spec((1, N_EXPERTS)),
                  pl.BlockSpec(memory_space=pl.ANY), pl.BlockSpec(memory_space=pl.ANY)],
        out_specs=[row(), pl.BlockSpec((tm, d // 2), this),
                   pl.BlockSpec((tm, V_DIM), prev),
                   pl.BlockSpec((2 * TOP_K, tm), lambda i: (0, jnp.maximum(i - 1, 0))), cnt()],
        out_shape=[jax.ShapeDtypeStruct((n, d), F32), jax.ShapeDtypeStruct((n, d // 2), jnp.int32),
                   jax.ShapeDtypeStruct((n, V_DIM), F32),
                   jax.ShapeDtypeStruct((2 * TOP_K, n), jnp.int32),
                   jax.ShapeDtypeStruct((1, N_EXPERTS), F32)],
        scratch_shapes=[pltpu.VMEM((1, N_EXPERTS), F32), pltpu.VMEM((tm, N_EXPERTS), F32)],
        compiler_params=_cparams(("arbitrary",), _post_vmem_mib(tm, d)),
        name="post",
    )(cnt_in, x, on, sga, gcc, wap_b, wout_b, g, wr_b, br, k_out, v_out)


def _moe_kernel(te_ref, tv_ref, x_ref, wgu_ref, bg_ref, bu_ref, wd_ref, bd_ref, y_ref, wgu_sc, wd_sc):
    t = pl.program_id(0)
    valid = tv_ref[t] > 0
    new_expert = (t == 0) | (te_ref[t] != te_ref[jnp.maximum(t - 1, 0)])
    f = wd_ref.shape[1]
    pair = 2 * V_DIM

    @pl.when(valid & new_expert)
    def _():
        r = lax.broadcasted_iota(jnp.int32, (pair, pair), 0)
        c = lax.broadcasted_iota(jnp.int32, (pair, pair), 1)
        src = jnp.where(c < V_DIM, 2 * c, 2 * (c - V_DIM) + 1)
        perm = (r == src).astype(BF16)
        for b in range(2 * f // pair):
            blk = wgu_ref[0, :, b * pair:(b + 1) * pair].astype(BF16)
            wgu_sc[:, b * pair:(b + 1) * pair] = _dot(blk, perm).astype(BF16)
        wd_sc[...] = wd_ref[0].astype(BF16)

    def expert_mlp(m):
        x_lo, x_hi = _unpack_halves(x_ref[0:m, :])
        x = jnp.concatenate([x_lo.astype(BF16), x_hi.astype(BF16)], axis=1)
        hgu = _dot(x, wgu_sc[...])
        acts = []
        for b in range(f // V_DIM):
            cols = slice(b * V_DIM, (b + 1) * V_DIM)
            gate = jnp.minimum(hgu[:, b * pair:b * pair + V_DIM] + bg_ref[0, :, cols], SWIGLU_LIMIT)
            up = jnp.clip(hgu[:, b * pair + V_DIM:(b + 1) * pair] + bu_ref[0, :, cols],
                          -SWIGLU_LIMIT, SWIGLU_LIMIT)
            glu = gate * _sigmoid(SWIGLU_ALPHA * gate)
            acts.append((glu * (up + 1.0)).astype(BF16))
        y = _dot(jnp.concatenate(acts, axis=1), wd_sc[...]) + bd_ref[0]
        y_ref[0:m, :] = _pack_halves(y)

    tile = x_ref.shape[0]
    for m in range(MOE_ROWS_STEP, tile + 1, MOE_ROWS_STEP):
        @pl.when((tv_ref[t] > m - MOE_ROWS_STEP) & (tv_ref[t] <= m))
        def _(m=m):
            expert_mlp(m)


def _moe_vmem_mib(tile, d, f):
    weights = 2 * (d * 2 * f + f * d) * 4 + (d * 2 * f + f * d) * 2
    rows = 2 * 2 * tile * (d // 2) * 4
    temporaries = tile * 2 * f * 4 + tile * f * 2 + tile * d * 2
    return -(-(weights + rows + temporaries) // MIB) + 4


def _moe(tile_expert, tile_valid, xs, w_gu, bg, bu, w_down, bd):
    p = xs.shape[0]
    f, d = w_down.shape[1:]
    n_tiles = p // MOE_TILE
    row = pl.BlockSpec((MOE_TILE, d // 2), lambda i, te, tv: (i, 0))
    wspec = lambda a, b_: pl.BlockSpec((1, a, b_), lambda i, te, tv: (te[i], 0, 0))
    return pl.pallas_call(
        _moe_kernel,
        grid_spec=pltpu.PrefetchScalarGridSpec(
            num_scalar_prefetch=2,
            grid=(n_tiles,),
            in_specs=[row, wspec(d, 2 * f), wspec(1, f), wspec(1, f), wspec(f, d), wspec(1, d)],
            out_specs=row,
            scratch_shapes=[pltpu.VMEM((d, 2 * f), BF16), pltpu.VMEM((f, d), BF16)]),
        out_shape=jax.ShapeDtypeStruct((p, d // 2), jnp.int32),
        compiler_params=_cparams(("arbitrary",), _moe_vmem_mib(MOE_TILE, d, f)),
        name="moe",
    )(tile_expert, tile_valid, xs, w_gu, bg, bu, w_down, bd)


def _sc_split(n):
    per_w = n // SC_WORKERS
    assert per_w * SC_WORKERS == n and per_w % 8 == 0, n
    ch = max(c for c in range(8, SC_MAX_CHUNK + 1, 8) if per_w % c == 0)
    return per_w, ch


def _sc_positions(pos):
    per_w, ch = _sc_split(pos.shape[1])
    return pos.reshape(TOP_K, SC_WORKERS, per_w // ch, ch).transpose(1, 2, 0, 3)


def _sc_mesh():
    return plsc.VectorSubcoreMesh(core_axis_name="c", subcore_axis_name="s",
                                  num_cores=SC_CORES, num_subcores=SC_SUBCORES)


def _sc_worker():
    return lax.axis_index("s") * SC_CORES + lax.axis_index("c")


def _sc_dispatch(srcs, poss, p_rows):
    w = srcs[0].shape[1]
    geo = [_sc_split(s.shape[0]) for s in srcs]
    nseg = len(srcs)
    scratch = []
    for _, ch in geo:
        scratch += [pltpu.VMEM((TOP_K, ch), jnp.int32), pltpu.VMEM((ch, w), jnp.int32)]
    scratch.append(pltpu.SemaphoreType.DMA)

    def body(*refs):
        src_refs, pos_refs, xs_ref = refs[:nseg], refs[nseg:2 * nseg], refs[2 * nseg]
        scr = refs[2 * nseg + 1:]
        sem = scr[-1]
        wid = _sc_worker()
        for g in range(nseg):
            per_w, ch = geo[g]

            @pl.loop(0, per_w // ch)
            def _(c, src=src_refs[g], pos=pos_refs[g], idx_v=scr[2 * g], rows_v=scr[2 * g + 1],
                  per_w=per_w, ch=ch):
                base = pl.multiple_of(wid * per_w + c * ch, 8)
                pltpu.sync_copy(pos.at[wid, c], idx_v)
                pltpu.sync_copy(src.at[pl.ds(base, ch)], rows_v)
                copies = [pltpu.async_copy(rows_v, xs_ref.at[idx_v.at[kk]], sem) for kk in range(TOP_K)]
                for cp in copies:
                    cp.wait()

    return pl.kernel(body, out_type=jax.ShapeDtypeStruct((p_rows, w), jnp.int32), mesh=_sc_mesh(),
                     scratch_types=scratch, name="sc_dispatch")(*srcs, *poss)


def _sc_gather(ys, poss, seg_rows):
    w = ys.shape[1]
    geo = [_sc_split(n) for n in seg_rows]
    nseg = len(seg_rows)
    row0 = [sum(seg_rows[:g]) for g in range(nseg)]
    scratch = []
    for _, ch in geo:
        scratch += [pltpu.VMEM((TOP_K, ch), jnp.int32), pltpu.VMEM((TOP_K, ch, w), jnp.int32)]
    scratch += [pltpu.SemaphoreType.DMA, pltpu.SemaphoreType.DMA]

    def body(*refs):
        ys_ref, pos_refs, out_ref = refs[0], refs[1:1 + nseg], refs[1 + nseg]
        scr = refs[2 + nseg:]
        sem_g, sem_w = scr[-2], scr[-1]
        wid = _sc_worker()
        for g in range(nseg):
            per_w, ch = geo[g]

            @pl.loop(0, per_w // ch)
            def _(c, pos=pos_refs[g], idx_v=scr[2 * g], rows_v=scr[2 * g + 1], per_w=per_w, ch=ch,
                  r0=row0[g]):
                base = pl.multiple_of(r0 + wid * per_w + c * ch, 8)
                pltpu.sync_copy(pos.at[wid, c], idx_v)
                gathers = [pltpu.async_copy(ys_ref.at[idx_v.at[kk]], rows_v.at[kk], sem_g)
                           for kk in range(TOP_K)]
                writes = []
                for kk in range(TOP_K):
                    gathers[kk].wait()
                    writes.append(pltpu.async_copy(rows_v.at[kk], out_ref.at[kk, pl.ds(base, ch)], sem_w))
                for cp in writes:
                    cp.wait()

    return pl.kernel(body, out_type=jax.ShapeDtypeStruct((TOP_K, sum(seg_rows), w), jnp.int32),
                     mesh=_sc_mesh(), scratch_types=scratch, name="sc_gather")(ys, *poss)


def _final_kernel(h_ref, yg_ref, tw_ref, g_ref, o_ref):
    tw = tw_ref[...]
    acc = None
    for kk in range(TOP_K):
        lo, hi = _unpack_halves(yg_ref[kk])
        part = jnp.concatenate([lo, hi], axis=1) * tw[:, kk:kk + 1]
        acc = part if acc is None else acc + part
    h = h_ref[...] + acc
    ms = jnp.mean(h * h, axis=-1, keepdims=True)
    o_ref[...] = (h * lax.rsqrt(ms + EPS)) * g_ref[...]


def _final(h, yg, tw, g, row0):
    n, d = h.shape
    tm = min(n, STREAM_TILE)
    assert n % tm == 0 and row0 % tm == 0
    blk0 = row0 // tm
    return pl.pallas_call(
        _final_kernel,
        grid=(n // tm,),
        in_specs=[pl.BlockSpec((tm, d), lambda i: (i, 0)),
                  pl.BlockSpec((TOP_K, tm, d // 2), lambda i: (0, i + blk0, 0)),
                  pl.BlockSpec((tm, V_DIM), lambda i: (i, 0)),
                  _const_spec((1, d))],
        out_specs=pl.BlockSpec((tm, d), lambda i: (i, 0)),
        out_shape=jax.ShapeDtypeStruct((n, d), F32),
        compiler_params=_cparams(("arbitrary",), -(-2 * tm * (2 * d * 4 + TOP_K * d * 2 + V_DIM * 4) // MIB) + 6),
        name="final",
    )(h, yg, tw, g)


def kernel(x_prompt, x_sample, cache_k, cache_v, state_conv, g_attn_norm, w_in, lambda_q, lambda_k,
           g_subln, conv_w, w_attn_proj, w_conv_proj, w_out, g_ffn_norm, w_router, b_router, w_gu, b_gu,
           w_down, b_down, g_final):
    assert w_in.shape[0] == 1, "single-layer trunk"
    bp, tp, d = x_prompt.shape
    bs, ts, _ = x_sample.shape
    past = cache_k.shape[2]
    f = w_down.shape[2]
    n_p, n_s = bp * tp, bs * ts
    n_all = n_p + n_s

    w_in_b = w_in[0].astype(BF16)
    wcp_b = w_conv_proj[0].astype(BF16)
    wap_b = w_attn_proj[0].astype(BF16)
    wout_b = w_out[0].astype(BF16)
    wr_b = w_router[0].astype(BF16)
    bgu = b_gu[0].reshape(N_EXPERTS, 1, f, 2)
    bg, bu = bgu[..., 0], bgu[..., 1]
    bd = b_down[0].reshape(N_EXPERTS, 1, d)
    g_attn = g_attn_norm[0].reshape(1, d)
    g_ffn = g_ffn_norm[0].reshape(1, d)
    gs = g_subln[0].reshape(1, V_DIM)
    lq, lk = lambda_q[0], lambda_k[0]
    br = b_router[0].reshape(1, N_EXPERTS)

    prev_p = jnp.zeros((bp, 8, d), F32)
    prev_s = jnp.pad(state_conv[0], ((0, 0), (6, 0), (0, 0)))

    q_p, k_p, v_p, kb_p, vb_p, sga_p, gcc_p, conv_p = _inproj(x_prompt, prev_p, g_attn, w_in_b, conv_w[0], wcp_b)
    outs_s = _inproj(x_sample.reshape(1, n_s, d), prev_s.reshape(1, 8 * bs, d), g_attn, w_in_b, conv_w[0], wcp_b)
    q_s, k_s, v_s, kb_s, vb_s, sga_s, gcc_s = (a.reshape(bs, ts, d) for a in outs_s[:7])
    conv_s = outs_s[7].reshape(bs, 2, d)

    on_p = _attn_prompt(lq, lk, gs, q_p, kb_p, vb_p)
    on_s = _attn_sample(lq, lk, gs, q_s, cache_k, cache_v, kb_s, vb_s)

    flat = lambda a, n: a.reshape(n, d)
    cnt0 = jnp.zeros((1, N_EXPERTS), F32)
    heads = lambda a: a.reshape(1, *a.shape[:2], N_HEADS, V_DIM)
    k_p, v_p, k_s, v_s = heads(k_p), heads(v_p), heads(k_s), heads(v_s)
    h_p, hn_p, tw_p, route_p, cnt_p = _post(cnt0, flat(x_prompt, n_p), flat(on_p, n_p), flat(sga_p, n_p),
                                            flat(gcc_p, n_p), wap_b, wout_b, g_ffn, wr_b, br, k_p, v_p)
    h_s, hn_s, tw_s, route_s, cnt = _post(cnt_p, flat(x_sample, n_s), flat(on_s, n_s), flat(sga_s, n_s),
                                          flat(gcc_s, n_s), wap_b, wout_b, g_ffn, wr_b, br, k_s, v_s)

    counts = cnt[0].astype(jnp.int32)
    padded = ((counts + MOE_TILE - 1) // MOE_TILE) * MOE_TILE
    ends = jnp.cumsum(padded)
    starts = ends - padded
    n_tiles = (n_all * TOP_K) // MOE_TILE + N_EXPERTS
    tile_row = jnp.arange(n_tiles, dtype=jnp.int32) * MOE_TILE
    tile_expert = jnp.sum((tile_row[:, None] >= ends[None, :]).astype(jnp.int32), axis=1)
    tile_expert = jnp.minimum(tile_expert, N_EXPERTS - 1)
    group_end = jnp.sum(jnp.where(tile_expert[:, None] == jnp.arange(N_EXPERTS, dtype=jnp.int32)[None, :],
                                  (starts + counts)[None, :], 0), axis=1)
    tile_valid = jnp.clip(group_end - tile_row, 0, MOE_TILE)
    tile_valid = jnp.where(tile_row < ends[-1], tile_valid, 0).astype(jnp.int32)

    def sorted_rows(route):
        experts = jnp.arange(N_EXPERTS, dtype=jnp.int32)[:, None, None]
        start = jnp.sum(jnp.where(route[None, :TOP_K] == experts, starts[:, None, None], 0), axis=0)
        return start + route[TOP_K:]

    poss = [_sc_positions(sorted_rows(route)) for route in (route_p, route_s)]
    xs = _sc_dispatch([hn_p, hn_s], poss, n_tiles * MOE_TILE)
    ys = _moe(tile_expert, tile_valid, xs, w_gu[0], bg, bu, w_down[0], bd)
    yg = _sc_gather(ys, poss, [n_p, n_s])

    y_p = _final(h_p, yg, tw_p, g_final.reshape(1, d), 0)
    y_s = _final(h_s, yg, tw_s, g_final.reshape(1, d), n_p)

    return (y_p.reshape(bp, tp, d), y_s.reshape(bs, ts, d), k_p, v_p, conv_p[None], k_s, v_s, conv_s[None])
```

```python
import functools
import math

import jax
import jax.numpy as jnp
from jax import lax
from jax.experimental import pallas as pl
from jax.experimental.pallas import tpu as pltpu
from jax.experimental.pallas import tpu_sc as plsc

N_HEADS = 8
HEAD_DIM = 64
V_DIM = 2 * HEAD_DIM
CHUNK = 64
N_EXPERTS = 32
TOP_K = 4
SWIGLU_LIMIT = 7.0
SWIGLU_ALPHA = 1.702
EPS = 1e-6
NEG = -1e30
LAMBDA_INIT = 0.8 - 0.6 * math.exp(-0.3 * 0)
N_SEG = 8
Q_SCALE = HEAD_DIM ** -0.5 * math.log2(math.e)

ROW_TILE = 512
STREAM_TILE = 1024
ATTN_TILE = 1024
MOE_TILE = 1024
MOE_ROWS_STEP = 512
MIB = 1024 * 1024
SC_CORES = 2
SC_SUBCORES = 16
SC_WORKERS = SC_CORES * SC_SUBCORES
SC_DISPATCH_CHUNK = 64
SC_GATHER_CHUNK = 32

BF16 = jnp.bfloat16
F32 = jnp.float32


def _cparams(semantics, vmem_mib):
    return pltpu.CompilerParams(dimension_semantics=semantics, vmem_limit_bytes=vmem_mib * MIB)


def _const_spec(shape):
    nd = len(shape)
    return pl.BlockSpec(shape, lambda *_: (0,) * nd, pipeline_mode=pl.Buffered(1))


def _sigmoid(x):
    return 1.0 / (1.0 + jnp.exp(-x))


def _dot(a, b):
    return jnp.dot(a, b, preferred_element_type=F32)


def _dot_nt(a, b):
    return lax.dot_general(a, b, (((1,), (1,)), ((), ())), preferred_element_type=F32)


def _pack_halves(x):
    w = x.shape[1] // 2
    lo = lax.bitcast_convert_type(x[:, :w].astype(BF16).astype(F32), jnp.uint32)
    hi = lax.bitcast_convert_type(x[:, w:].astype(BF16).astype(F32), jnp.uint32)
    return lax.bitcast_convert_type((lo >> 16) | (hi & jnp.uint32(0xFFFF0000)), jnp.int32)


def _unpack_halves(words):
    u = lax.bitcast_convert_type(words, jnp.uint32)
    lo = lax.bitcast_convert_type(u << 16, F32)
    hi = lax.bitcast_convert_type(u & jnp.uint32(0xFFFF0000), F32)
    return lo, hi


def _inproj_kernel(x_ref, prev_ref, g_ref, w_ref, cw_ref, wcp_ref,
                   q_ref, k32_ref, v32_ref, kb_ref, vb_ref, sga_ref, gcc_ref, cnew_ref, ubuf):
    t = pl.program_id(1)
    tm, d = x_ref.shape[1], x_ref.shape[2]
    n_str = prev_ref.shape[1] // 8
    rows_s = tm // n_str
    base = [s * (rows_s + 8) for s in range(n_str)]

    @pl.when(t == 0)
    def _():
        for s in range(n_str):
            ubuf[base[s]:base[s] + 8, :] = prev_ref[0, 8 * s:8 * s + 8, :]

    x = x_ref[0]
    ms = jnp.mean(x * x, axis=-1, keepdims=True)
    xn = ((x * lax.rsqrt(ms + EPS)) * g_ref[...]).astype(BF16)

    def seg(i):
        return _dot(xn, w_ref[:, i * d:(i + 1) * d])

    q_ref[0] = (seg(0) * Q_SCALE).astype(BF16)
    k = seg(1)
    k32_ref[0] = k
    kb_ref[0] = k.astype(BF16)
    v = seg(2)
    v32_ref[0] = v
    vb_ref[0] = v.astype(BF16)

    u = seg(5) * seg(3)
    convs = []
    for s in range(n_str):
        u_s = u[s * rows_s:(s + 1) * rows_s]
        ubuf[base[s] + 8:base[s] + 8 + rows_s, :] = u_s
        convs.append(ubuf[base[s] + 6:base[s] + 6 + rows_s, :] * cw_ref[0:1, :]
                     + ubuf[base[s] + 7:base[s] + 7 + rows_s, :] * cw_ref[1:2, :] + u_s * cw_ref[2:3, :])
    conv = convs[0] if n_str == 1 else jnp.concatenate(convs, axis=0)
    cpre = (seg(4) * conv).astype(BF16)
    c_out = _dot(cpre, wcp_ref[...])
    gcc_ref[0] = (_sigmoid(seg(7)) * c_out).astype(BF16)
    sga_ref[0] = _sigmoid(seg(6)).astype(BF16)
    for s in range(n_str):
        cnew_ref[0, 2 * s:2 * s + 2, :] = ubuf[base[s] + rows_s + 6:base[s] + rows_s + 8, :]
        ubuf[base[s]:base[s] + 8, :] = ubuf[base[s] + rows_s:base[s] + rows_s + 8, :]


def _inproj(x, conv_prev8, g, w_in_b, conv_w, wcp_b):
    b, t, d = x.shape
    n_str = conv_prev8.shape[1] // 8
    tm = min(t, ROW_TILE)
    assert t % tm == 0 and tm % (8 * n_str) == 0 and (n_str == 1 or tm == t)
    row = lambda: pl.BlockSpec((1, tm, d), lambda i, j: (i, j, 0))
    bf = jax.ShapeDtypeStruct((b, t, d), BF16)
    f32 = jax.ShapeDtypeStruct((b, t, d), F32)
    return pl.pallas_call(
        _inproj_kernel,
        grid=(b, t // tm),
        in_specs=[row(),
                  pl.BlockSpec((1, 8 * n_str, d), lambda i, j: (i, 0, 0)),
                  _const_spec((1, d)),
                  _const_spec((d, N_SEG * d)),
                  _const_spec((3, d)),
                  _const_spec((d, d))],
        out_specs=[row(), row(), row(), row(), row(), row(), row(),
                   pl.BlockSpec((1, 2 * n_str, d), lambda i, j: (i, 0, 0))],
        out_shape=[bf, f32, f32, bf, bf, bf, bf, jax.ShapeDtypeStruct((b, 2 * n_str, d), F32)],
        scratch_shapes=[pltpu.VMEM((tm + 8 * n_str, d), F32)],
        compiler_params=_cparams(("arbitrary", "arbitrary"), 56),
        name="inproj",
    )(x, conv_prev8, g, w_in_b, conv_w, wcp_b)


def _lambda(lq_ref, lk_ref):
    e = jnp.exp(jnp.sum(lq_ref[...] * lk_ref[...], axis=-1, keepdims=True))
    return e[0:1, :] - e[1:2, :] + LAMBDA_INIT


def _stack_maps(q):
    lane = lax.broadcasted_iota(jnp.int32, q.shape, 1)
    zero = jnp.zeros_like(q)
    return jnp.concatenate([jnp.where(lane < HEAD_DIM, q, zero),
                            jnp.where(lane >= HEAD_DIM, q, zero)], axis=0)


def _attn_finish(acc, l, lam, gs, tq):
    o = acc / l
    o = o[:tq] - lam * o[tq:]
    ms = jnp.mean(o * o, axis=-1, keepdims=True)
    return ((o * lax.rsqrt(ms + EPS)) * gs) * (1.0 - LAMBDA_INIT)


def _attn_prompt_kernel(lq_ref, lk_ref, gs_ref, q_ref, k_ref, v_ref, o_ref,
                        s_sc, p_sc, qq_sc, m_sc, l_sc, acc_sc, bias_sc, *, tile):
    n_q = q_ref.shape[1] // tile
    lane_tiles = [slice(c * V_DIM, (c + 1) * V_DIM) for c in range(tile // V_DIM)]
    lam = _lambda(lq_ref, lk_ref)

    def tile_rows(j):
        return pl.ds(j * tile if isinstance(j, int) else pl.multiple_of(j * tile, tile), tile)

    def rows_of(ref, j):
        return ref[0, tile_rows(j), :]

    @pl.when((pl.program_id(0) == 0) & (pl.program_id(1) == 0))
    def _():
        r = lax.broadcasted_iota(jnp.int32, bias_sc.shape, 0)
        c = lax.broadcasted_iota(jnp.int32, bias_sc.shape, 1)
        r = jnp.where(r >= tile, r - tile, r)
        bias_sc[...] = jnp.where((c // CHUNK) <= (r // CHUNK), 0.0, NEG).astype(F32)

    def key_tile(j, p):
        if isinstance(p, int):
            return j if p == 0 else p - 1
        return jnp.where(p == 0, j, p - 1)

    def diag_scores(j):
        qq = _stack_maps(rows_of(q_ref, j))
        qq_sc[...] = qq
        s_sc[...] = _dot_nt(qq, rows_of(k_ref, j)) + bias_sc[...]

    def scores(kt):
        s_sc[...] = _dot_nt(qq_sc[...], rows_of(k_ref, kt))

    def probs_v(kt):
        return _dot(p_sc[...], rows_of(v_ref, kt))

    def softmax(pv):
        m_cur = s_sc[:, lane_tiles[0]]
        for lt in lane_tiles[1:]:
            m_cur = jnp.maximum(m_cur, s_sc[:, lt])
        m_cur = jnp.max(m_cur, axis=-1, keepdims=True)
        if pv is None:
            m_new = jnp.broadcast_to(m_cur, m_sc.shape)
        else:
            m_old = m_sc[...]
            m_new = jnp.maximum(m_old, m_cur)
            alpha = jnp.exp2(m_old - m_new)
        psum = None
        for lt in lane_tiles:
            pc = jnp.exp2(s_sc[:, lt] - m_new)
            p_sc[:, lt] = pc.astype(BF16)
            psum = pc if psum is None else psum + pc
        if pv is None:
            l_sc[...] = psum
            acc_sc[...] = jnp.zeros(acc_sc.shape, F32)
        else:
            l_sc[...] = alpha * l_sc[...] + psum
            acc_sc[...] = alpha * (acc_sc[...] + pv)
        m_sc[...] = m_new

    def finish(j, last_kt):
        acc = acc_sc[...] + probs_v(last_kt)
        l = jnp.sum(l_sc[...], axis=-1, keepdims=True)
        o_ref[0, tile_rows(j), :] = _attn_finish(
            acc, l, lam, gs_ref[...], tile).astype(BF16)

    diag_scores(0)
    softmax(None)
    diag_scores(min(1, n_q - 1))

    def query_tile(j, carry):
        finish(j - 1, key_tile(j - 1, j - 1))
        softmax(None)
        scores(0)

        def body(p, c):
            pv = probs_v(key_tile(j, p - 1))
            softmax(pv)
            scores(p)
            return c

        lax.fori_loop(1, j, body, 0)
        pv = probs_v(key_tile(j, j - 1))
        softmax(pv)
        diag_scores(jnp.minimum(j + 1, n_q - 1))
        return carry

    lax.fori_loop(1, n_q, query_tile, 0)
    finish(n_q - 1, key_tile(n_q - 1, n_q - 1))


def _attn_vmem_mib(t, tile):
    rows = 2 * tile
    score_f32 = rows * tile * 4
    scratch = (2 * score_f32 + score_f32 // 2 + 3 * rows * V_DIM * 4
               + rows * V_DIM * 2)
    blocks = 2 * 4 * (t * V_DIM * 2)
    temporaries = score_f32
    return -(-(scratch + blocks + temporaries) // MIB) + 6


def _attn_prompt(lq, lk, gs, q, kb, vb):
    b, t, d = q.shape
    tile = min(t, ATTN_TILE)
    assert t % tile == 0 and tile % CHUNK == 0 and tile % V_DIM == 0
    spec = pl.BlockSpec((1, t, V_DIM), lambda bi, h: (bi, 0, h))
    rows = 2 * tile
    return pl.pallas_call(
        functools.partial(_attn_prompt_kernel, tile=tile),
        grid=(b, N_HEADS),
        in_specs=[_const_spec((2, HEAD_DIM)), _const_spec((2, HEAD_DIM)), _const_spec((1, V_DIM)),
                  spec, spec, spec],
        out_specs=spec,
        out_shape=jax.ShapeDtypeStruct((b, t, d), BF16),
        scratch_shapes=[pltpu.VMEM((rows, tile), F32), pltpu.VMEM((rows, tile), BF16),
                        pltpu.VMEM((rows, V_DIM), BF16),
                        pltpu.VMEM((rows, V_DIM), F32), pltpu.VMEM((rows, V_DIM), F32),
                        pltpu.VMEM((rows, V_DIM), F32), pltpu.VMEM((rows, tile), F32)],
        compiler_params=_cparams(("arbitrary", "arbitrary"), _attn_vmem_mib(t, tile)),
        name="attn_prompt",
    )(lq, lk, gs, q, kb, vb)


def _attn_sample_kernel(lq_ref, lk_ref, gs_ref, q_ref, ck_ref, cv_ref, kn_ref, vn_ref, o_ref):
    tq = q_ref.shape[1]
    lam = _lambda(lq_ref, lk_ref)
    heads = [slice(h * V_DIM, (h + 1) * V_DIM) for h in range(N_HEADS)]
    qqs = [_stack_maps(q_ref[0, :, cols]) for cols in heads]
    s_ps = [_dot_nt(qq, ck_ref[:, h, :].astype(BF16)) for h, qq in enumerate(qqs)]
    s_ns = [_dot_nt(qq, kn_ref[0, :, cols]) for qq, cols in zip(qqs, heads)]
    ms = [jnp.maximum(jnp.max(s_p, axis=-1, keepdims=True), jnp.max(s_n, axis=-1, keepdims=True))
          for s_p, s_n in zip(s_ps, s_ns)]
    p_ps = [jnp.exp2(s_p - m) for s_p, m in zip(s_ps, ms)]
    p_ns = [jnp.exp2(s_n - m) for s_n, m in zip(s_ns, ms)]
    ls = [jnp.sum(p_p, axis=-1, keepdims=True) + jnp.sum(p_n, axis=-1, keepdims=True)
          for p_p, p_n in zip(p_ps, p_ns)]
    accs = [_dot(p_p.astype(BF16), cv_ref[:, h, :].astype(BF16)) + _dot(p_n.astype(BF16), vn_ref[0, :, cols])
            for h, (p_p, p_n, cols) in enumerate(zip(p_ps, p_ns, heads))]
    for cols, acc, l in zip(heads, accs, ls):
        o_ref[0, :, cols] = _attn_finish(acc, l, lam, gs_ref[...], tq).astype(BF16)


def _attn_sample(lq, lk, gs, q, cache_k, cache_v, kb, vb):
    b, t, d = q.shape
    past = cache_k.shape[2]
    qspec = pl.BlockSpec((1, t, d), lambda bi: (bi, 0, 0))
    cspec = pl.BlockSpec((None, None, past, N_HEADS, V_DIM), lambda bi: (0, bi, 0, 0, 0))
    return pl.pallas_call(
        _attn_sample_kernel,
        grid=(b,),
        in_specs=[_const_spec((2, HEAD_DIM)), _const_spec((2, HEAD_DIM)), _const_spec((1, V_DIM)),
                  qspec, cspec, cspec, qspec, qspec],
        out_specs=qspec,
        out_shape=jax.ShapeDtypeStruct((b, t, d), BF16),
        compiler_params=_cparams(("arbitrary",), 40),
        name="attn_sample",
    )(lq, lk, gs, q, cache_k, cache_v, kb, vb)


def _lane_cols(cols, dtype):
    rows = cols[0].shape[0]
    lane = lax.broadcasted_iota(jnp.int32, (rows, V_DIM), 1)
    out = jnp.zeros((rows, V_DIM), dtype)
    for kk, col in enumerate(cols):
        out = jnp.where(lane == kk, col.astype(dtype), out)
    return out


def _post_kernel(cnt_in_ref, x_ref, on_ref, sga_ref, gcc_ref, wap_ref, wout_ref, g_ref, wr_ref, br_ref,
                 k_out_ref, v_out_ref, h_ref, hn_ref, tw_ref, route_ref, cnt_ref, carry, logits_sc):
    i = pl.program_id(0)
    tm = x_ref.shape[0]

    @pl.when(i == 0)
    def _():
        carry[...] = cnt_in_ref[...]
        logits_sc[...] = jnp.zeros(logits_sc.shape, F32)

    logits = logits_sc[...]
    a_out = _dot(on_ref[...], wap_ref[...])

    lane = lax.broadcasted_iota(jnp.int32, logits.shape, 1)
    work = logits
    vals, idxs, sels = [], [], []
    for _ in range(TOP_K):
        mk = jnp.max(work, axis=-1, keepdims=True)
        ik = jnp.min(jnp.where(work == mk, lane, N_EXPERTS), axis=-1, keepdims=True)
        sel = lane == ik
        work = jnp.where(sel, -jnp.inf, work)
        vals.append(mk)
        idxs.append(ik)
        sels.append(sel)
    es = [jnp.exp(vk - vals[0]) for vk in vals]
    denom = es[0] + es[1] + es[2] + es[3]
    tw_ref[...] = _lane_cols([e / denom for e in es], F32)

    merged = sga_ref[...].astype(F32) * a_out + gcc_ref[...].astype(F32)
    h = x_ref[...] + _dot(merged.astype(BF16), wout_ref[...])
    h_ref[...] = h

    picked = (sels[0] | sels[1] | sels[2] | sels[3]).astype(BF16)
    r = lax.broadcasted_iota(jnp.int32, (tm, tm), 0)
    c = lax.broadcasted_iota(jnp.int32, (tm, tm), 1)
    counts = carry[...]
    before = _dot((c < r).astype(BF16), picked) + counts
    ranks = [jnp.sum(jnp.where(sk, before, 0.0), axis=-1, keepdims=True) for sk in sels]
    route = _lane_cols(idxs + ranks, jnp.int32)
    route_ref[...] = route.T[0:2 * TOP_K, :]
    routed = jnp.where(i > 0, 1.0, 0.0)
    counts = counts + routed * jnp.sum(picked.astype(F32), axis=0, keepdims=True)
    carry[...] = counts
    cnt_ref[...] = counts

    ms = jnp.mean(h * h, axis=-1, keepdims=True)
    hn32 = (h * lax.rsqrt(ms + EPS)) * g_ref[...]
    hn_ref[...] = _pack_halves(hn32)
    logits_sc[...] = _dot(hn32.astype(BF16), wr_ref[...]) + br_ref[...]


def _post_vmem_mib(tm, d):
    blocks = 2 * (tm * d * (4 + 2 + 2 + 2 + 4 + 2) + tm * V_DIM * 4 + 2 * TOP_K * tm * 4)
    weights = 2 * d * d * 2 + d * N_EXPERTS * 2
    temporaries = 4 * tm * d * 4
    return -(-(blocks + weights + temporaries) // MIB) + 4


def _post(cnt_in, x, on, sga, gcc, wap_b, wout_b, g, wr_b, br, k_out, v_out):
    n, d = x.shape
    tm = min(n, STREAM_TILE)
    assert n % tm == 0
    n_tiles = n // tm
    this = lambda i: (jnp.minimum(i, n_tiles - 1), 0)
    prev = lambda i: (jnp.maximum(i - 1, 0), 0)
    row = lambda: pl.BlockSpec((tm, d), this)
    cnt = lambda: pl.BlockSpec((1, N_EXPERTS), lambda i: (0, 0))
    return pl.pallas_call(
        _post_kernel,
        grid=(n_tiles + 1,),
        in_specs=[cnt(), row(), row(), row(), row(),
                  _const_spec((d, d)), _const_spec((d, d)), _const_spec((1, d)),
                  _const_spec((d, N_EXPERTS)), _const_spec((1, N_EXPERTS)),
                  pl.BlockSpec(memory_space=pl.ANY), pl.BlockSpec(memory_space=pl.ANY)],
        out_specs=[row(), pl.BlockSpec((tm, d // 2), this),
                   pl.BlockSpec((tm, V_DIM), prev),
                   pl.BlockSpec((2 * TOP_K, tm), lambda i: (0, jnp.maximum(i - 1, 0))), cnt()],
        out_shape=[jax.ShapeDtypeStruct((n, d), F32), jax.ShapeDtypeStruct((n, d // 2), jnp.int32),
                   jax.ShapeDtypeStruct((n, V_DIM), F32),
                   jax.ShapeDtypeStruct((2 * TOP_K, n), jnp.int32),
                   jax.ShapeDtypeStruct((1, N_EXPERTS), F32)],
        scratch_shapes=[pltpu.VMEM((1, N_EXPERTS), F32), pltpu.VMEM((tm, N_EXPERTS), F32)],
        compiler_params=_cparams(("arbitrary",), _post_vmem_mib(tm, d)),
        name="post",
    )(cnt_in, x, on, sga, gcc, wap_b, wout_b, g, wr_b, br, k_out, v_out)


def _moe_kernel(te_ref, tv_ref, x_ref, wgu_ref, bg_ref, bu_ref, wd_ref, bd_ref, y_ref, wgu_sc, wd_sc):
    t = pl.program_id(0)
    valid = tv_ref[t] > 0
    new_expert = (t == 0) | (te_ref[t] != te_ref[jnp.maximum(t - 1, 0)])
    f = wd_ref.shape[1]
    pair = 2 * V_DIM

    @pl.when(valid & new_expert)
    def _():
        r = lax.broadcasted_iota(jnp.int32, (pair, pair), 0)
        c = lax.broadcasted_iota(jnp.int32, (pair, pair), 1)
        src = jnp.where(c < V_DIM, 2 * c, 2 * (c - V_DIM) + 1)
        perm = (r == src).astype(BF16)
        for b in range(2 * f // pair):
            blk = wgu_ref[0, :, b * pair:(b + 1) * pair].astype(BF16)
            wgu_sc[:, b * pair:(b + 1) * pair] = _dot(blk, perm).astype(BF16)
        wd_sc[...] = wd_ref[0].astype(BF16)

    def expert_mlp(m):
        x_lo, x_hi = _unpack_halves(x_ref[0:m, :])
        x = jnp.concatenate([x_lo.astype(BF16), x_hi.astype(BF16)], axis=1)
        hgu = _dot(x, wgu_sc[...])
        acts = []
        for b in range(f // V_DIM):
            cols = slice(b * V_DIM, (b + 1) * V_DIM)
            gate = jnp.minimum(hgu[:, b * pair:b * pair + V_DIM] + bg_ref[0, :, cols], SWIGLU_LIMIT)
            up = jnp.clip(hgu[:, b * pair + V_DIM:(b + 1) * pair] + bu_ref[0, :, cols],
                          -SWIGLU_LIMIT, SWIGLU_LIMIT)
            glu = gate * _sigmoid(SWIGLU_ALPHA * gate)
            acts.append((glu * (up + 1.0)).astype(BF16))
        y = _dot(jnp.concatenate(acts, axis=1), wd_sc[...]) + bd_ref[0]
        y_ref[0:m, :] = _pack_halves(y)

    tile = x_ref.shape[0]
    for m in range(MOE_ROWS_STEP, tile + 1, MOE_ROWS_STEP):
        @pl.when((tv_ref[t] > m - MOE_ROWS_STEP) & (tv_ref[t] <= m))
        def _(m=m):
            expert_mlp(m)


def _moe_vmem_mib(tile, d, f):
    weights = 2 * (d * 2 * f + f * d) * 4 + (d * 2 * f + f * d) * 2
    rows = 2 * 2 * tile * (d // 2) * 4
    temporaries = tile * 2 * f * 4 + tile * f * 2 + tile * d * 2
    return -(-(weights + rows + temporaries) // MIB) + 4


def _moe(tile_expert, tile_valid, xs, w_gu, bg, bu, w_down, bd):
    p = xs.shape[0]
    f, d = w_down.shape[1:]
    n_tiles = p // MOE_TILE
    row = pl.BlockSpec((MOE_TILE, d // 2), lambda i, te, tv: (i, 0))
    wspec = lambda a, b_: pl.BlockSpec((1, a, b_), lambda i, te, tv: (te[i], 0, 0))
    return pl.pallas_call(
        _moe_kernel,
        grid_spec=pltpu.PrefetchScalarGridSpec(
            num_scalar_prefetch=2,
            grid=(n_tiles,),
            in_specs=[row, wspec(d, 2 * f), wspec(1, f), wspec(1, f), wspec(f, d), wspec(1, d)],
            out_specs=row,
            scratch_shapes=[pltpu.VMEM((d, 2 * f), BF16), pltpu.VMEM((f, d), BF16)]),
        out_shape=jax.ShapeDtypeStruct((p, d // 2), jnp.int32),
        compiler_params=_cparams(("arbitrary",), _moe_vmem_mib(MOE_TILE, d, f)),
        name="moe",
    )(tile_expert, tile_valid, xs, w_gu, bg, bu, w_down, bd)


def _sc_split(n, max_chunk):
    per_w = n // SC_WORKERS
    assert per_w * SC_WORKERS == n and per_w % 8 == 0, n
    ch = max(c for c in range(8, max_chunk + 1, 8) if per_w % c == 0)
    return per_w, ch


def _sc_positions(pos, max_chunk):
    per_w, ch = _sc_split(pos.shape[1], max_chunk)
    return pos.reshape(TOP_K, SC_WORKERS, per_w // ch, ch).transpose(1, 2, 0, 3)


def _sc_mesh():
    return plsc.VectorSubcoreMesh(core_axis_name="c", subcore_axis_name="s",
                                  num_cores=SC_CORES, num_subcores=SC_SUBCORES)


def _sc_worker():
    return lax.axis_index("s") * SC_CORES + lax.axis_index("c")


def _sc_dispatch(srcs, poss, p_rows):
    w = srcs[0].shape[1]
    geo = [_sc_split(s.shape[0], SC_DISPATCH_CHUNK) for s in srcs]
    nseg = len(srcs)
    scratch = []
    for _, ch in geo:
        scratch += [pltpu.VMEM((TOP_K, ch), jnp.int32), pltpu.VMEM((ch, w), jnp.int32)]
    scratch.append(pltpu.SemaphoreType.DMA)

    def body(*refs):
        src_refs, pos_refs, xs_ref = refs[:nseg], refs[nseg:2 * nseg], refs[2 * nseg]
        scr = refs[2 * nseg + 1:]
        sem = scr[-1]
        wid = _sc_worker()
        for g in range(nseg):
            per_w, ch = geo[g]

            @pl.loop(0, per_w // ch)
            def _(c, src=src_refs[g], pos=pos_refs[g], idx_v=scr[2 * g], rows_v=scr[2 * g + 1],
                  per_w=per_w, ch=ch):
                base = pl.multiple_of(wid * per_w + c * ch, 8)
                pltpu.sync_copy(pos.at[wid, c], idx_v)
                pltpu.sync_copy(src.at[pl.ds(base, ch)], rows_v)
                copies = [pltpu.async_copy(rows_v, xs_ref.at[idx_v.at[kk]], sem) for kk in range(TOP_K)]
                for cp in copies:
                    cp.wait()

    return pl.kernel(body, out_type=jax.ShapeDtypeStruct((p_rows, w), jnp.int32), mesh=_sc_mesh(),
                     scratch_types=scratch, name="sc_dispatch")(*srcs, *poss)


def _sc_gather(ys, poss, seg_rows):
    w = ys.shape[1]
    geo = [_sc_split(n, SC_GATHER_CHUNK) for n in seg_rows]
    nseg = len(seg_rows)
    row0 = [sum(seg_rows[:g]) for g in range(nseg)]
    scratch = []
    for _, ch in geo:
        scratch += [pltpu.VMEM((TOP_K, ch), jnp.int32), pltpu.VMEM((TOP_K, ch, w), jnp.int32)]
    scratch += [pltpu.SemaphoreType.DMA((TOP_K,)), pltpu.SemaphoreType.DMA((TOP_K,))]

    def body(*refs):
        ys_ref, pos_refs, out_ref = refs[0], refs[1:1 + nseg], refs[1 + nseg]
        scr = refs[2 + nseg:]
        sem_g, sem_w = scr[-2], scr[-1]
        wid = _sc_worker()
        for g in range(nseg):
            per_w, ch = geo[g]

            @pl.loop(0, per_w // ch)
            def _(c, pos=pos_refs[g], idx_v=scr[2 * g], rows_v=scr[2 * g + 1], per_w=per_w, ch=ch,
                  r0=row0[g]):
                base = pl.multiple_of(r0 + wid * per_w + c * ch, 8)
                pltpu.sync_copy(pos.at[wid, c], idx_v)
                gathers = [pltpu.async_copy(ys_ref.at[idx_v.at[kk]], rows_v.at[kk], sem_g.at[kk])
                           for kk in range(TOP_K)]
                writes = []
                for kk in range(TOP_K):
                    gathers[kk].wait()
                    writes.append(pltpu.async_copy(rows_v.at[kk], out_ref.at[kk, pl.ds(base, ch)],
                                                   sem_w.at[kk]))
                for cp in writes:
                    cp.wait()

    return pl.kernel(body, out_type=jax.ShapeDtypeStruct((TOP_K, sum(seg_rows), w), jnp.int32),
                     mesh=_sc_mesh(), scratch_types=scratch, name="sc_gather")(ys, *poss)


def _final_kernel(h_ref, yg_ref, tw_ref, g_ref, o_ref):
    tw = tw_ref[...]
    acc = None
    for kk in range(TOP_K):
        lo, hi = _unpack_halves(yg_ref[kk])
        part = jnp.concatenate([lo, hi], axis=1) * tw[:, kk:kk + 1]
        acc = part if acc is None else acc + part
    h = h_ref[...] + acc
    ms = jnp.mean(h * h, axis=-1, keepdims=True)
    o_ref[...] = (h * lax.rsqrt(ms + EPS)) * g_ref[...]


def _final(h, yg, tw, g, row0):
    n, d = h.shape
    tm = min(n, STREAM_TILE)
    assert n % tm == 0 and row0 % tm == 0
    blk0 = row0 // tm
    return pl.pallas_call(
        _final_kernel,
        grid=(n // tm,),
        in_specs=[pl.BlockSpec((tm, d), lambda i: (i, 0)),
                  pl.BlockSpec((TOP_K, tm, d // 2), lambda i: (0, i + blk0, 0)),
                  pl.BlockSpec((tm, V_DIM), lambda i: (i, 0)),
                  _const_spec((1, d))],
        out_specs=pl.BlockSpec((tm, d), lambda i: (i, 0)),
        out_shape=jax.ShapeDtypeStruct((n, d), F32),
        compiler_params=_cparams(("arbitrary",), -(-2 * tm * (2 * d * 4 + TOP_K * d * 2 + V_DIM * 4) // MIB) + 6),
        name="final",
    )(h, yg, tw, g)


def kernel(x_prompt, x_sample, cache_k, cache_v, state_conv, g_attn_norm, w_in, lambda_q, lambda_k,
           g_subln, conv_w, w_attn_proj, w_conv_proj, w_out, g_ffn_norm, w_router, b_router, w_gu, b_gu,
           w_down, b_down, g_final):
    assert w_in.shape[0] == 1, "single-layer trunk"
    bp, tp, d = x_prompt.shape
    bs, ts, _ = x_sample.shape
    f = w_down.shape[2]
    n_p, n_s = bp * tp, bs * ts
    n_all = n_p + n_s

    w_in_b = w_in[0].astype(BF16)
    wcp_b = w_conv_proj[0].astype(BF16)
    wap_b = w_attn_proj[0].astype(BF16)
    wout_b = w_out[0].astype(BF16)
    wr_b = w_router[0].astype(BF16)
    bgu = b_gu[0].reshape(N_EXPERTS, 1, f, 2)
    bg, bu = bgu[..., 0], bgu[..., 1]
    bd = b_down[0].reshape(N_EXPERTS, 1, d)
    g_attn = g_attn_norm[0].reshape(1, d)
    g_ffn = g_ffn_norm[0].reshape(1, d)
    gs = g_subln[0].reshape(1, V_DIM)
    lq, lk = lambda_q[0], lambda_k[0]
    br = b_router[0].reshape(1, N_EXPERTS)

    prev_p = jnp.zeros((bp, 8, d), F32)
    prev_s = jnp.pad(state_conv[0], ((0, 0), (6, 0), (0, 0)))

    q_p, k_p, v_p, kb_p, vb_p, sga_p, gcc_p, conv_p = _inproj(x_prompt, prev_p, g_attn, w_in_b, conv_w[0], wcp_b)
    outs_s = _inproj(x_sample.reshape(1, n_s, d), prev_s.reshape(1, 8 * bs, d), g_attn, w_in_b, conv_w[0], wcp_b)
    q_s, k_s, v_s, kb_s, vb_s, sga_s, gcc_s = (a.reshape(bs, ts, d) for a in outs_s[:7])
    conv_s = outs_s[7].reshape(bs, 2, d)

    on_p = _attn_prompt(lq, lk, gs, q_p, kb_p, vb_p)
    on_s = _attn_sample(lq, lk, gs, q_s, cache_k, cache_v, kb_s, vb_s)

    flat = lambda a, n: a.reshape(n, d)
    cnt0 = jnp.zeros((1, N_EXPERTS), F32)
    heads = lambda a: a.reshape(1, *a.shape[:2], N_HEADS, V_DIM)
    k_p, v_p, k_s, v_s = heads(k_p), heads(v_p), heads(k_s), heads(v_s)
    h_p, hn_p, tw_p, route_p, cnt_p = _post(cnt0, flat(x_prompt, n_p), flat(on_p, n_p), flat(sga_p, n_p),
                                            flat(gcc_p, n_p), wap_b, wout_b, g_ffn, wr_b, br, k_p, v_p)
    h_s, hn_s, tw_s, route_s, cnt = _post(cnt_p, flat(x_sample, n_s), flat(on_s, n_s), flat(sga_s, n_s),
                                          flat(gcc_s, n_s), wap_b, wout_b, g_ffn, wr_b, br, k_s, v_s)

    counts = cnt[0].astype(jnp.int32)
    padded = ((counts + MOE_TILE - 1) // MOE_TILE) * MOE_TILE
    ends = jnp.cumsum(padded)
    starts = ends - padded
    n_tiles = (n_all * TOP_K) // MOE_TILE + N_EXPERTS
    tile_row = jnp.arange(n_tiles, dtype=jnp.int32) * MOE_TILE
    tile_expert = jnp.sum((tile_row[:, None] >= ends[None, :]).astype(jnp.int32), axis=1)
    tile_expert = jnp.minimum(tile_expert, N_EXPERTS - 1)
    group_end = jnp.sum(jnp.where(tile_expert[:, None] == jnp.arange(N_EXPERTS, dtype=jnp.int32)[None, :],
                                  (starts + counts)[None, :], 0), axis=1)
    tile_valid = jnp.clip(group_end - tile_row, 0, MOE_TILE)
    tile_valid = jnp.where(tile_row < ends[-1], tile_valid, 0).astype(jnp.int32)

    def sorted_rows(route):
        experts = jnp.arange(N_EXPERTS, dtype=jnp.int32)[:, None, None]
        start = jnp.sum(jnp.where(route[None, :TOP_K] == experts, starts[:, None, None], 0), axis=0)
        return start + route[TOP_K:]

    rows = [sorted_rows(route) for route in (route_p, route_s)]
    xs = _sc_dispatch([hn_p, hn_s], [_sc_positions(r, SC_DISPATCH_CHUNK) for r in rows], n_tiles * MOE_TILE)
    ys = _moe(tile_expert, tile_valid, xs, w_gu[0], bg, bu, w_down[0], bd)
    yg = _sc_gather(ys, [_sc_positions(r, SC_GATHER_CHUNK) for r in rows], [n_p, n_s])

    y_p = _final(h_p, yg, tw_p, g_final.reshape(1, d), 0)
    y_s = _final(h_s, yg, tw_s, g_final.reshape(1, d), n_p)

    return (y_p.reshape(bp, tp, d), y_s.reshape(bs, ts, d), k_p, v_p, conv_p[None], k_s, v_s, conv_s[None])
```
